```python
import math
import jax, jax.numpy as jnp
from jax import lax
import numpy as np

D_MODEL = 1024
BATCH = 4
SEQ = 8192
DEPTH = 2

GLA_HEADS = 4
GLA_DK = 64
GLA_DV = 128
GLA_GATE_RANK = 16
GLA_TAU = 16.0
GLA_CHUNK = 64
MLA_HEADS = 4
MLA_NOPE = 128
MLA_ROPE = 64
MLA_V = 128
MLA_Q_RANK = 256
MLA_KV_RANK = 128
ROPE_THETA = 10000.0
ATTN_BLOCK = 128
D_GLA = GLA_HEADS * GLA_DV
D_MLA = MLA_HEADS * MLA_V
D_MIX = D_GLA + D_MLA
D_FF = 4 * D_MODEL
N_MOD = 6
RMS_EPS = 1e-6
IN_SIZES = (GLA_HEADS * GLA_DK, GLA_HEADS * GLA_DK, D_GLA, D_GLA, GLA_GATE_RANK,
            MLA_Q_RANK, MLA_KV_RANK, MLA_ROPE)
D_IN = sum(IN_SIZES)

kernel_name = 'hybrid_gla_mla_adaln_block'


def split_cols(t, sizes):
    idx = []
    acc = 0
    for s in sizes[:-1]:
        acc += s
        idx.append(acc)
    return jnp.split(t, idx, axis=-1)


def rms_norm(x, gain=None):
    xf = x.astype(jnp.float32)
    y = xf * lax.rsqrt(jnp.mean(xf * xf, axis=-1, keepdims=True) + RMS_EPS)
    if gain is not None:
        y = y * gain.astype(jnp.float32)
    return y.astype(x.dtype)


def rope_tables(positions):
    inv_freq = ROPE_THETA ** (-jnp.arange(0, MLA_ROPE, 2, dtype=jnp.float32) / MLA_ROPE)
    ang = positions.astype(jnp.float32)[..., None] * inv_freq
    return jnp.cos(ang), jnp.sin(ang)


def apply_rope(t, cos, sin):
    t = t.astype(jnp.float32)
    half = t.shape[-1] // 2
    t1, t2 = t[..., :half], t[..., half:]
    return jnp.concatenate([t1 * cos - t2 * sin, t2 * cos + t1 * sin], axis=-1)


def gla_chunked(q, k, v, log_a):
    B, S, H, dk = q.shape
    dv = v.shape[-1]
    C = GLA_CHUNK
    N = S // C

    def to_chunks(t):
        return t.reshape(B, N, C, H, t.shape[-1]).transpose(1, 0, 3, 2, 4)

    causal = jnp.tril(jnp.ones((C, C), dtype=bool))

    def step(state, inp):
        qc, kc, vc, gc = inp
        b = jnp.cumsum(gc, axis=2)
        o_inter = jnp.einsum('bhcd,bhde->bhce', qc * jnp.exp(b), state)
        diff = b[:, :, :, None, :] - b[:, :, None, :, :]
        decay = jnp.exp(jnp.where(causal[:, :, None], diff, -jnp.inf))
        attn = jnp.sum(qc[:, :, :, None, :] * kc[:, :, None, :, :] * decay, axis=-1)
        o_intra = jnp.einsum('bhij,bhje->bhie', attn, vc)
        b_last = b[:, :, -1:, :]
        k_dec = kc * jnp.exp(b_last - b)
        state = jnp.exp(b_last[:, :, 0, :])[..., None] * state + jnp.einsum('bhcd,bhce->bhde', k_dec, vc)
        return state, o_inter + o_intra

    state0 = jnp.zeros((B, H, dk, dv), dtype=jnp.float32)
    _, o = lax.scan(step, state0, (to_chunks(q), to_chunks(k), to_chunks(v), to_chunks(log_a)))
    return o.transpose(1, 0, 3, 2, 4).reshape(B, S, H, dv)


def mla_attention(q_nope, q_pe, k_nope, k_pe, v):
    B, S, H, _ = q_nope.shape
    NB = S // ATTN_BLOCK
    scale = (MLA_NOPE + MLA_ROPE) ** -0.5
    key_idx = jnp.arange(S)

    def blocks(t):
        return t.reshape(B, NB, ATTN_BLOCK, *t.shape[2:]).swapaxes(0, 1)

    def one_block(args):
        qn, qp, blk = args
        s = (jnp.einsum('bqhd,bkhd->bhqk', qn, k_nope).astype(jnp.float32)
             + jnp.einsum('bqhr,bkr->bhqk', qp, k_pe).astype(jnp.float32)) * scale
        q_idx = blk * ATTN_BLOCK + jnp.arange(ATTN_BLOCK)
        mask = key_idx[None, :] <= q_idx[:, None]
        p = jax.nn.softmax(jnp.where(mask, s, -jnp.inf), axis=-1)
        return jnp.einsum('bhqk,bkhe->bqhe', p.astype(v.dtype), v)

    out = lax.map(one_block, (blocks(q_nope), blocks(q_pe), jnp.arange(NB)))
    return out.swapaxes(0, 1).reshape(B, S, H, v.shape[-1])


def hybrid_mixer(h, cos, sin, w_in, w_gate_up, b_gate, gla_out_norm, q_a_norm, w_q_up,
                 kv_a_norm, w_kv_up, q_norm_nope, k_norm_nope, q_norm_rope, k_norm_rope, w_out):
    B, S, _ = h.shape
    proj = h @ w_in
    g_q, g_k, g_v, g_o, g_a, m_q, m_kv, m_kpe = split_cols(proj, IN_SIZES)

    q = g_q.reshape(B, S, GLA_HEADS, GLA_DK).astype(jnp.float32) * (GLA_DK ** -0.5)
    k = g_k.reshape(B, S, GLA_HEADS, GLA_DK).astype(jnp.float32)
    v = g_v.reshape(B, S, GLA_HEADS, GLA_DV).astype(jnp.float32)
    log_a = jax.nn.log_sigmoid((g_a @ w_gate_up + b_gate).astype(jnp.float32)) / GLA_TAU
    log_a = log_a.reshape(B, S, GLA_HEADS, GLA_DK)
    o = gla_chunked(q, k, v, log_a)
    o = rms_norm(o, gla_out_norm) * jax.nn.silu(g_o.reshape(B, S, GLA_HEADS, GLA_DV).astype(jnp.float32))
    gla_out = o.reshape(B, S, D_GLA).astype(h.dtype)

    qh = (rms_norm(m_q, q_a_norm) @ w_q_up).reshape(B, S, MLA_HEADS, MLA_NOPE + MLA_ROPE)
    q_nope, q_pe = qh[..., :MLA_NOPE], qh[..., MLA_NOPE:]
    q_nope = rms_norm(q_nope, q_norm_nope)
    q_pe = apply_rope(rms_norm(q_pe, q_norm_rope), cos[:, :, None, :], sin[:, :, None, :])
    kv = (rms_norm(m_kv, kv_a_norm) @ w_kv_up).reshape(B, S, MLA_HEADS, MLA_NOPE + MLA_V)
    k_nope, mv = kv[..., :MLA_NOPE], kv[..., MLA_NOPE:]
    k_nope = rms_norm(k_nope, k_norm_nope)
    k_pe = apply_rope(rms_norm(m_kpe, k_norm_rope), cos, sin)
    mla_out = mla_attention(q_nope, q_pe, k_nope, k_pe, mv).reshape(B, S, D_MLA)

    mixed = jnp.concatenate([gla_out, mla_out.astype(gla_out.dtype)], axis=-1)
    return mixed @ w_out


def setup_inputs(seed: int = 0) -> dict:
    key = jax.random.key(seed)
    ks = jax.random.split(key, 24)
    f32 = jnp.float32

    def w(k, shape, fan_in):
        return jax.random.normal(k, shape, f32) * (fan_in ** -0.5)

    def gain(k, shape):
        return 1.0 + 0.02 * jax.random.normal(k, shape, f32)

    x = jax.random.normal(ks[0], (BATCH, SEQ, D_MODEL), f32)
    c = jax.random.normal(ks[1], (BATCH, D_MODEL), f32)
    offsets = jax.random.randint(ks[2], (BATCH, 1), 0, 1024, dtype=jnp.int32)
    positions = (offsets + jnp.arange(SEQ, dtype=jnp.int32)[None, :]).astype(jnp.int32)
    return {
        'x': x,
        'c': c,
        'positions': positions,
        'w_ada': w(ks[3], (DEPTH, D_MODEL, N_MOD * D_MODEL), D_MODEL),
        'b_ada': 0.02 * jax.random.normal(ks[4], (DEPTH, N_MOD * D_MODEL), f32),
        'w_in': w(ks[5], (DEPTH, D_MODEL, D_IN), D_MODEL),
        'w_gate_up': w(ks[6], (DEPTH, GLA_GATE_RANK, GLA_HEADS * GLA_DK), GLA_GATE_RANK),
        'b_gate': 0.1 * jax.random.normal(ks[7], (DEPTH, GLA_HEADS * GLA_DK), f32),
        'gla_out_norm': gain(ks[8], (DEPTH, GLA_DV)),
        'q_a_norm': gain(ks[9], (DEPTH, MLA_Q_RANK)),
        'w_q_up': w(ks[10], (DEPTH, MLA_Q_RANK, MLA_HEADS * (MLA_NOPE + MLA_ROPE)), MLA_Q_RANK),
        'kv_a_norm': gain(ks[11], (DEPTH, MLA_KV_RANK)),
        'w_kv_up': w(ks[12], (DEPTH, MLA_KV_RANK, MLA_HEADS * (MLA_NOPE + MLA_V)), MLA_KV_RANK),
        'q_norm_nope': gain(ks[13], (DEPTH, MLA_NOPE)),
        'k_norm_nope': gain(ks[14], (DEPTH, MLA_NOPE)),
        'q_norm_rope': gain(ks[15], (DEPTH, MLA_ROPE)),
        'k_norm_rope': gain(ks[16], (DEPTH, MLA_ROPE)),
        'w_out': w(ks[17], (DEPTH, D_MIX, D_MODEL), D_MIX),
        'w_mlp_up': w(ks[18], (DEPTH, D_MODEL, D_FF), D_MODEL),
        'w_mlp_down': w(ks[19], (DEPTH, D_FF, D_MODEL), D_FF),
    }


def reference(x, c, positions, w_ada, b_ada, w_in, w_gate_up, b_gate, gla_out_norm, q_a_norm,
              w_q_up, kv_a_norm, w_kv_up, q_norm_nope, k_norm_nope, q_norm_rope, k_norm_rope,
              w_out, w_mlp_up, w_mlp_down):
    cos, sin = rope_tables(positions)
    cond = jax.nn.silu(c)
    for l in range(DEPTH):
        mod = (cond @ w_ada[l] + b_ada[l])[:, None, :]
        shift_a, scale_a, gate_a, shift_f, scale_f, gate_f = jnp.split(mod, N_MOD, axis=-1)
        h = rms_norm(x) * (1.0 + scale_a) + shift_a
        mix = hybrid_mixer(h, cos, sin, w_in[l], w_gate_up[l], b_gate[l], gla_out_norm[l],
                           q_a_norm[l], w_q_up[l], kv_a_norm[l], w_kv_up[l], q_norm_nope[l],
                           k_norm_nope[l], q_norm_rope[l], k_norm_rope[l], w_out[l])
        x = x + gate_a * mix
        h = rms_norm(x) * (1.0 + scale_f) + shift_f
        x = x + gate_f * (jnp.square(jax.nn.relu(h @ w_mlp_up[l])) @ w_mlp_down[l])
    return x
```

```python
import functools

import numpy as np
import jax
import jax.numpy as jnp
from jax import lax
from jax.experimental import pallas as pl
from jax.experimental.pallas import tpu as pltpu

F32 = jnp.float32
BF16 = jnp.bfloat16

D_MODEL = 1024
GLA_HEADS = 4
GLA_DK = 64
GLA_DV = 128
GLA_GATE_RANK = 16
GLA_TAU = 16.0
MLA_HEADS = 4
MLA_NOPE = 128
MLA_ROPE = 64
MLA_V = 128
MLA_Q_RANK = 256
MLA_KV_RANK = 128
ROPE_THETA = 10000.0
D_GLA_QK = GLA_HEADS * GLA_DK
D_GLA = GLA_HEADS * GLA_DV
D_MLA = MLA_HEADS * MLA_V
D_FF = 4 * D_MODEL
N_MOD = 6
RMS_EPS = 1e-6
D_IN_PADDED = 2048
MLA_QK_PAD = 256

LANES = 128
VMEM_LIMIT_BYTES = 56 * 1024 * 1024

PRE_TM = 512
POST_TM = 512
FF_CHUNK = 1024
GLA_CHUNK = 128
GLA_BLOCK = 512
ATTN_TQ = 512
ADA_TN = 1536
ROPE_ROWS = 2048
NEG_BIG = -1e30


def _dot(a, b):
    return jnp.dot(a, b, preferred_element_type=F32)


def _dot_nt(a, b):
    return lax.dot_general(a, b, (((1,), (1,)), ((), ())), preferred_element_type=F32)


def _dot_tn(a, b):
    return lax.dot_general(a, b, (((0,), (0,)), ((), ())), preferred_element_type=F32)


def _split2(a):
    hi = a.astype(BF16)
    lo = (a - hi.astype(F32)).astype(BF16)
    return hi, lo


def _dot_f32(a, b):
    ah, al = _split2(a)
    bh, bl = _split2(b)
    return _dot(ah, bh) + _dot(ah, bl) + _dot(al, bh)


def _rms(x):
    return x * lax.rsqrt(jnp.mean(x * x, axis=-1, keepdims=True) + RMS_EPS)


def _const_spec(shape):
    zeros = (0,) * len(shape)
    return pl.BlockSpec(shape, lambda *_: zeros, pipeline_mode=pl.Buffered(1))


def _params(*semantics):
    return pltpu.CompilerParams(dimension_semantics=semantics,
                                vmem_limit_bytes=VMEM_LIMIT_BYTES)


def _adaln_kernel(c_ref, w_ref, b_ref, o_ref):
    c = c_ref[...]
    cond = c * jax.nn.sigmoid(c)
    o_ref[0] = _dot_f32(cond, w_ref[0]) + b_ref[0]


def _adaln(c_pad, w_ada, b_ada):
    depth, d, n = w_ada.shape
    rows = c_pad.shape[0]
    return pl.pallas_call(
        _adaln_kernel,
        grid=(depth, n // ADA_TN),
        in_specs=[
            pl.BlockSpec((rows, d), lambda l, j: (0, 0)),
            pl.BlockSpec((1, d, ADA_TN), lambda l, j: (l, 0, j)),
            pl.BlockSpec((1, 1, ADA_TN), lambda l, j: (l, 0, j)),
        ],
        out_specs=pl.BlockSpec((1, rows, ADA_TN), lambda l, j: (l, 0, j)),
        out_shape=jax.ShapeDtypeStruct((depth, rows, n), F32),
        compiler_params=_params("arbitrary", "arbitrary"),
        name="adaln",
    )(c_pad, w_ada, b_ada.reshape(depth, 1, n))


def _rope_kernel(pos_ref, inv_ref, cos_ref, sin_ref):
    ang = pos_ref[...].astype(F32) * inv_ref[...]
    cos_ref[...] = jnp.cos(ang)
    sin_ref[...] = jnp.sin(ang)


def _rope_tables(positions):
    t = positions.size
    half = MLA_ROPE // 2
    per_row = LANES // half
    rows = t // per_row
    inv_freq = ROPE_THETA ** (-jnp.arange(0, MLA_ROPE, 2, dtype=F32) / MLA_ROPE)
    pos_rep = jnp.repeat(positions.reshape(rows, per_row), half, axis=1)
    inv_rep = jnp.tile(inv_freq, per_row).reshape(1, LANES)
    tr = min(ROPE_ROWS, rows)
    cos, sin = pl.pallas_call(
        _rope_kernel,
        grid=(rows // tr,),
        in_specs=[pl.BlockSpec((tr, LANES), lambda i: (i, 0)),
                  pl.BlockSpec((1, LANES), lambda i: (0, 0))],
        out_specs=[pl.BlockSpec((tr, LANES), lambda i: (i, 0))] * 2,
        out_shape=[jax.ShapeDtypeStruct((rows, LANES), F32)] * 2,
        compiler_params=_params("arbitrary"),
        name="rope_tables",
    )(pos_rep, inv_rep)
    cos = jnp.tile(cos.reshape(t, half), (1, per_row))
    sin = jnp.tile(sin.reshape(t, half), (1, per_row))
    return cos, sin


def _pre_kernel(x_ref, mod_ref, cos_ref, sin_ref, w_in_ref, wg_ref, bg_ref,
                qan_ref, kvan_ref, qnn_ref, knn_ref, qnr_ref, knr_ref, wq_ref, wkv_ref,
                gq_ref, gk_ref, gv_ref, go_ref, la_ref, q_ref, k_ref, v_ref):
    mod = mod_ref[0]
    h = _rms(x_ref[...]) * (1.0 + mod[1:2, :]) + mod[0:1, :]
    proj = _dot(h.astype(BF16), w_in_ref[...])

    gq_ref[...] = proj[:, 0:256] * (GLA_DK ** -0.5)
    gk_ref[...] = proj[:, 256:512]
    gv_ref[...] = proj[:, 512:1024].astype(BF16)
    go_ref[...] = proj[:, 1024:1536].astype(BF16)
    tail = proj[:, 1920:2048]
    z = _dot_f32(tail, wg_ref[...]) + bg_ref[...]
    log_sig = jnp.minimum(z, 0.0) - jnp.log1p(jnp.exp(-jnp.abs(z)))
    la_ref[...] = log_sig * (1.0 / GLA_TAU)

    lane = lax.broadcasted_iota(jnp.int32, (1, LANES), 1)
    low = lane < MLA_ROPE
    first_half = (lane % MLA_ROPE) < (MLA_ROPE // 2)
    cos = cos_ref[...]
    sin = jnp.where(first_half, -sin_ref[...], sin_ref[...])
    scale = (MLA_NOPE + MLA_ROPE) ** -0.5

    def rope(y):
        rot = jnp.where(first_half, pltpu.roll(y, LANES - MLA_ROPE // 2, 1),
                        pltpu.roll(y, MLA_ROPE // 2, 1))
        return y * cos + rot * sin

    def pair_rms(y):
        sq = y * y
        s_lo = jnp.sum(jnp.where(low, sq, 0.0), axis=-1, keepdims=True)
        s_hi = jnp.sum(jnp.where(low, 0.0, sq), axis=-1, keepdims=True)
        ms = jnp.where(low, s_lo, s_hi) * (1.0 / MLA_ROPE)
        return y * lax.rsqrt(ms + RMS_EPS)

    mq = _rms(proj[:, 1536:1792]) * qan_ref[...]
    qh = _dot(mq.astype(BF16), wq_ref[...])
    q_rope = []
    for p in range(MLA_HEADS // 2):
        blk = qh[:, 512 + LANES * p:512 + LANES * (p + 1)]
        q_rope.append(rope(pair_rms(blk) * qnr_ref[...]) * scale)
    for hd in range(MLA_HEADS):
        qn = _rms(qh[:, LANES * hd:LANES * (hd + 1)]) * qnn_ref[...] * scale
        q_ref[:, MLA_QK_PAD * hd:MLA_QK_PAD * hd + LANES] = qn.astype(BF16)
        keep = low if hd % 2 == 0 else jnp.logical_not(low)
        qr = jnp.where(keep, q_rope[hd // 2], 0.0)
        q_ref[:, MLA_QK_PAD * hd + LANES:MLA_QK_PAD * (hd + 1)] = qr.astype(BF16)

    mkv = _rms(proj[:, 1792:1920]) * kvan_ref[...]
    kv = _dot(mkv.astype(BF16), wkv_ref[...])
    v_ref[...] = kv[:, 512:1024].astype(BF16)
    sq = tail * tail
    ms = jnp.sum(jnp.where(low, sq, 0.0), axis=-1, keepdims=True) * (1.0 / MLA_ROPE)
    kr_even = rope(tail * lax.rsqrt(ms + RMS_EPS) * knr_ref[...])
    kr_odd = pltpu.roll(kr_even, MLA_ROPE, 1)
    for hd in range(MLA_HEADS):
        kn = _rms(kv[:, LANES * hd:LANES * (hd + 1)]) * knn_ref[...]
        k_ref[:, MLA_QK_PAD * hd:MLA_QK_PAD * hd + LANES] = kn.astype(BF16)
        kr = kr_even if hd % 2 == 0 else kr_odd
        k_ref[:, MLA_QK_PAD * hd + LANES:MLA_QK_PAD * (hd + 1)] = kr.astype(BF16)


def _pre_mixer(x2, mod, cos, sin, lw, seq):
    t = x2.shape[0]
    tm = min(PRE_TM, seq)
    per_batch = seq // tm
    row = lambda i: (i, 0)
    out_widths = [(D_GLA_QK, F32), (D_GLA_QK, F32), (D_GLA, BF16), (D_GLA, BF16), (D_GLA_QK, F32),
                  (MLA_HEADS * MLA_QK_PAD, BF16), (MLA_HEADS * MLA_QK_PAD, BF16), (D_MLA, BF16)]
    return pl.pallas_call(
        _pre_kernel,
        grid=(t // tm,),
        in_specs=[
            pl.BlockSpec((tm, D_MODEL), row),
            pl.BlockSpec((1, N_MOD, D_MODEL), lambda i: (i // per_batch, 0, 0)),
            pl.BlockSpec((tm, LANES), row),
            pl.BlockSpec((tm, LANES), row),
            _const_spec((D_MODEL, D_IN_PADDED)),
            _const_spec((LANES, D_GLA_QK)),
            _const_spec((1, D_GLA_QK)),
            _const_spec((1, MLA_Q_RANK)),
            _const_spec((1, MLA_KV_RANK)),
            _const_spec((1, MLA_NOPE)),
            _const_spec((1, MLA_NOPE)),
            _const_spec((1, LANES)),
            _const_spec((1, LANES)),
            _const_spec((MLA_Q_RANK, MLA_HEADS * (MLA_NOPE + MLA_ROPE))),
            _const_spec((MLA_KV_RANK, MLA_HEADS * (MLA_NOPE + MLA_V))),
        ],
        out_specs=[pl.BlockSpec((tm, w), row) for w, _ in out_widths],
        out_shape=[jax.ShapeDtypeStruct((t, w), dt) for w, dt in out_widths],
        compiler_params=_params("arbitrary"),
        name="pre_mixer",
    )(x2, mod, cos, sin, lw["w_in"], lw["wg"], lw["bg"], lw["qan"], lw["kvan"], lw["qnn"],
      lw["knn"], lw["qnr"], lw["knr"], lw["wq"], lw["wkv"])


def _gla_constants(c):
    t = np.arange(c)[:, None]
    u = np.arange(c)[None, :]
    sums = [np.zeros((c, c), bool)]
    masks = [t == u]
    s = 1
    while s < c:
        same = (t // s) == (u // s)
        odd = ((t // s) % 2) == 1
        sums.append(np.where(odd, same & (u <= t), same & (u > t)))
        masks.append(odd & ((u // s) == (t // s) - 1))
        s *= 2
    sums.append(u <= t)
    return (jnp.asarray(np.concatenate(sums, 0), BF16),
            jnp.asarray(np.stack(masks), F32))


def _gla_kernel(q_ref, k_ref, v_ref, go_ref, la_ref, msum_ref, mask_ref, gain_ref,
                o_ref, state_ref, *, chunk, n_chunks, blocks_per_seq):
    n_levels = mask_ref.shape[0]
    pairs = GLA_HEADS // 2

    @pl.when(pl.program_id(0) % blocks_per_seq == 0)
    def _():
        state_ref[...] = jnp.zeros_like(state_ref)

    lane = lax.broadcasted_iota(jnp.int32, (1, LANES), 1)
    low = lane < GLA_DK
    srow = lax.broadcasted_iota(jnp.int32, (2 * GLA_DV, LANES), 0) // GLA_DV
    scol = lax.broadcasted_iota(jnp.int32, (2 * GLA_DV, LANES), 1) // GLA_DK
    diag_blocks = srow == scol

    def one_chunk(ci, carry):
        rows = pl.ds(pl.multiple_of(ci * chunk, chunk), chunk)
        la_hi, la_lo = _split2(la_ref[rows, :])
        msum = msum_ref[...]
        dsum = _dot(msum, la_hi) + _dot(msum, la_lo)
        b = dsum[n_levels * chunk:(n_levels + 1) * chunk]
        b_last = b[chunk - 1:chunk, :]
        q = q_ref[rows, :]
        k = k_ref[rows, :]
        q_dec = (q * jnp.exp(b)).astype(BF16)
        k_dec = (k * jnp.exp(b_last - b)).astype(BF16)
        decay = jnp.exp(b_last)

        for p in range(pairs):
            cols = slice(LANES * p, LANES * (p + 1))
            q_p = q[:, cols]
            k_lo = jnp.where(low, k[:, cols], 0.0)
            k_hi = jnp.where(low, 0.0, k[:, cols])
            att = [jnp.zeros((chunk, chunk), F32), jnp.zeros((chunk, chunk), F32)]
            for lv in range(n_levels):
                if lv == 0:
                    qs, ks = q_p.astype(BF16), (k_lo.astype(BF16), k_hi.astype(BF16))
                else:
                    e = jnp.exp(dsum[lv * chunk:(lv + 1) * chunk, cols])
                    qs = (q_p * e).astype(BF16)
                    ks = ((k_lo * e).astype(BF16), (k_hi * e).astype(BF16))
                m = mask_ref[lv]
                for hh in range(2):
                    att[hh] = att[hh] + _dot_nt(qs, ks[hh]) * m

            st = state_ref[p]
            vcols = slice(2 * GLA_DV * p, 2 * GLA_DV * (p + 1))
            v_p = v_ref[rows, vcols]
            o_inter = _dot_nt(q_dec[:, cols], st.astype(BF16))
            kv = _dot_tn(v_p, k_dec[:, cols])
            state_ref[p] = jnp.where(diag_blocks, decay[:, cols] * st + kv, 0.0)

            for hh in range(2):
                hd = 2 * p + hh
                hc = slice(GLA_DV * hd, GLA_DV * (hd + 1))
                o = o_inter[:, GLA_DV * hh:GLA_DV * (hh + 1)] + _dot(
                    att[hh].astype(BF16), v_p[:, GLA_DV * hh:GLA_DV * (hh + 1)])
                g = go_ref[rows, hc].astype(F32)
                o = _rms(o) * gain_ref[...] * (g * jax.nn.sigmoid(g))
                o_ref[rows, hc] = o.astype(BF16)
        return carry

    lax.fori_loop(0, n_chunks, one_chunk, 0)


def _gla(gq, gk, gv, go, la, gain, seq):
    t = gq.shape[0]
    chunk = min(GLA_CHUNK, seq)
    tb = min(GLA_BLOCK, seq)
    msum, masks = _gla_constants(chunk)
    row = lambda i: (i, 0)
    kern = functools.partial(_gla_kernel, chunk=chunk, n_chunks=tb // chunk,
                             blocks_per_seq=seq // tb)
    return pl.pallas_call(
        kern,
        grid=(t // tb,),
        in_specs=[
            pl.BlockSpec((tb, D_GLA_QK), row),
            pl.BlockSpec((tb, D_GLA_QK), row),
            pl.BlockSpec((tb, D_GLA), row),
            pl.BlockSpec((tb, D_GLA), row),
            pl.BlockSpec((tb, D_GLA_QK), row),
            _const_spec(msum.shape),
            _const_spec(masks.shape),
            _const_spec((1, GLA_DV)),
        ],
        out_specs=pl.BlockSpec((tb, D_GLA), row),
        out_shape=jax.ShapeDtypeStruct((t, D_GLA), BF16),
        scratch_shapes=[pltpu.VMEM((GLA_HEADS // 2, 2 * GLA_DV, LANES), F32)],
        compiler_params=_params("arbitrary"),
        name="gla_scan",
    )(gq, gk, gv, go, la, msum, masks, gain)


def _attn_kernel(q_ref, k_ref, v_ref, o_ref, m_ref, l_ref, acc_ref, *, tq):
    qi = pl.program_id(2)
    q = q_ref[...]
    m_ref[...] = jnp.full_like(m_ref, NEG_BIG)
    l_ref[...] = jnp.zeros_like(l_ref)
    acc_ref[...] = jnp.zeros_like(acc_ref)

    def step(j, masked):
        rows = pl.ds(pl.multiple_of(j * tq, tq), tq)
        s = _dot_nt(q, k_ref[rows, :])
        if masked:
            r = lax.broadcasted_iota(jnp.int32, (tq, tq), 0)
            c = lax.broadcasted_iota(jnp.int32, (tq, tq), 1)
            s = jnp.where(c <= r, s, NEG_BIG)
        m_prev = m_ref[...]
        m_new = jnp.maximum(m_prev, jnp.max(s, axis=-1, keepdims=True))
        alpha = jnp.exp(m_prev - m_new)
        p = jnp.exp(s - m_new)
        l_ref[...] = alpha * l_ref[...] + jnp.sum(p, axis=-1, keepdims=True)
        acc_ref[...] = alpha * acc_ref[...] + _dot(p.astype(BF16), v_ref[rows, :])
        m_ref[...] = m_new

    def full_step(j, carry):
        step(j, False)
        return carry

    lax.fori_loop(0, qi, full_step, 0)
    step(qi, True)
    o_ref[...] = (acc_ref[...] / l_ref[...]).astype(BF16)


def _mla_attention(q, k, v, batch, seq):
    t = q.shape[0]
    tq = min(ATTN_TQ, seq)
    nq = seq // tq
    return pl.pallas_call(
        functools.partial(_attn_kernel, tq=tq),
        grid=(batch, MLA_HEADS, nq),
        in_specs=[
            pl.BlockSpec((tq, MLA_QK_PAD), lambda b, h, i: (b * nq + i, h)),
            pl.BlockSpec((seq, MLA_QK_PAD), lambda b, h, i: (b, h)),
            pl.BlockSpec((seq, MLA_V), lambda b, h, i: (b, h)),
        ],
        out_specs=pl.BlockSpec((tq, MLA_V), lambda b, h, i: (b * nq + i, h)),
        out_shape=jax.ShapeDtypeStruct((t, D_MLA), BF16),
        scratch_shapes=[pltpu.VMEM((tq, 1), F32), pltpu.VMEM((tq, 1), F32),
                        pltpu.VMEM((tq, MLA_V), F32)],
        compiler_params=_params("arbitrary", "arbitrary", "arbitrary"),
        name="mla_attention",
    )(q, k, v)


def _post_kernel(x_ref, gla_ref, mla_ref, mod_ref, wo_ref, w1_ref, w2_ref, o_ref):
    mod = mod_ref[0]
    mix = _dot(gla_ref[...], wo_ref[0:D_GLA, :]) + _dot(mla_ref[...], wo_ref[D_GLA:, :])
    x1 = x_ref[...] + mod[2:3, :] * mix
    h = (_rms(x1) * (1.0 + mod[4:5, :]) + mod[3:4, :]).astype(BF16)
    acc = jnp.zeros_like(x1)
    for c in range(D_FF // FF_CHUNK):
        cols = slice(FF_CHUNK * c, FF_CHUNK * (c + 1))
        u = jnp.maximum(_dot(h, w1_ref[:, cols]), 0.0)
        acc = acc + _dot((u * u).astype(BF16), w2_ref[cols, :])
    o_ref[...] = x1 + mod[5:6, :] * acc


def _post_mixer(x2, gla, mla, mod, lw, seq):
    t = x2.shape[0]
    tm = min(POST_TM, seq)
    per_batch = seq // tm
    row = lambda i: (i, 0)
    return pl.pallas_call(
        _post_kernel,
        grid=(t // tm,),
        in_specs=[
            pl.BlockSpec((tm, D_MODEL), row),
            pl.BlockSpec((tm, D_GLA), row),
            pl.BlockSpec((tm, D_MLA), row),
            pl.BlockSpec((1, N_MOD, D_MODEL), lambda i: (i // per_batch, 0, 0)),
            _const_spec((D_GLA + D_MLA, D_MODEL)),
            _const_spec((D_MODEL, D_FF)),
            _const_spec((D_FF, D_MODEL)),
        ],
        out_specs=pl.BlockSpec((tm, D_MODEL), row),
        out_shape=jax.ShapeDtypeStruct((t, D_MODEL), F32),
        compiler_params=_params("arbitrary"),
        name="post_mixer",
    )(x2, gla, mla, mod, lw["wo"], lw["w1"], lw["w2"])


def _layer_weights(l, w_in, w_gate_up, b_gate, gla_out_norm, q_a_norm, w_q_up, kv_a_norm,
                   w_kv_up, q_norm_nope, k_norm_nope, q_norm_rope, k_norm_rope, w_out,
                   w_mlp_up, w_mlp_down):
    wi = w_in[l]
    w_in_p = jnp.concatenate(
        [wi[:, 0:1536], wi[:, 1552:1808], wi[:, 1808:1936], wi[:, 1936:2000], wi[:, 1536:1552],
         jnp.zeros((D_MODEL, D_IN_PADDED - 2000), F32)], axis=1).astype(BF16)
    wg = jnp.zeros((LANES, D_GLA_QK), F32).at[MLA_ROPE:MLA_ROPE + GLA_GATE_RANK].set(w_gate_up[l])
    wq = w_q_up[l].reshape(MLA_Q_RANK, MLA_HEADS, MLA_NOPE + MLA_ROPE)
    wq = jnp.concatenate([wq[:, :, :MLA_NOPE].reshape(MLA_Q_RANK, -1),
                          wq[:, :, MLA_NOPE:].reshape(MLA_Q_RANK, -1)], axis=1).astype(BF16)
    wkv = w_kv_up[l].reshape(MLA_KV_RANK, MLA_HEADS, MLA_NOPE + MLA_V)
    wkv = jnp.concatenate([wkv[:, :, :MLA_NOPE].reshape(MLA_KV_RANK, -1),
                           wkv[:, :, MLA_NOPE:].reshape(MLA_KV_RANK, -1)], axis=1).astype(BF16)
    return dict(
        w_in=w_in_p, wg=wg, bg=b_gate[l].reshape(1, -1),
        gla_gain=gla_out_norm[l].reshape(1, -1),
        qan=q_a_norm[l].reshape(1, -1), kvan=kv_a_norm[l].reshape(1, -1),
        qnn=q_norm_nope[l].reshape(1, -1), knn=k_norm_nope[l].reshape(1, -1),
        qnr=jnp.tile(q_norm_rope[l], 2).reshape(1, -1),
        knr=jnp.concatenate([k_norm_rope[l], jnp.zeros((LANES - MLA_ROPE,), F32)]).reshape(1, -1),
        wq=wq, wkv=wkv, wo=w_out[l].astype(BF16),
        w1=w_mlp_up[l].astype(BF16), w2=w_mlp_down[l].astype(BF16))


def kernel(x, c, positions, w_ada, b_ada, w_in, w_gate_up, b_gate, gla_out_norm, q_a_norm, w_q_up, kv_a_norm, w_kv_up, q_norm_nope, k_norm_nope, q_norm_rope, k_norm_rope, w_out, w_mlp_up, w_mlp_down):
    batch, seq, d = x.shape
    depth = w_ada.shape[0]
    cos, sin = _rope_tables(positions)
    c_pad = jnp.concatenate([c, jnp.zeros((8 - batch % 8 if batch % 8 else 0, d), F32)], axis=0)
    mods = _adaln(c_pad, w_ada, b_ada)[:, :batch].reshape(depth, batch, N_MOD, d)
    x2 = x.reshape(batch * seq, d)
    for l in range(depth):
        lw = _layer_weights(l, w_in, w_gate_up, b_gate, gla_out_norm, q_a_norm, w_q_up,
                            kv_a_norm, w_kv_up, q_norm_nope, k_norm_nope, q_norm_rope,
                            k_norm_rope, w_out, w_mlp_up, w_mlp_down)
        gq, gk, gv, go, la, q, k, v = _pre_mixer(x2, mods[l], cos, sin, lw, seq)
        gla = _gla(gq, gk, gv, go, la, lw["gla_gain"], seq)
        mla = _mla_attention(q, k, v, batch, seq)
        x2 = _post_mixer(x2, gla, mla, mods[l], lw, seq)
    return x2.reshape(batch, seq, d)
```

```python
import functools

import numpy as np
import jax
import jax.numpy as jnp
from jax import lax
from jax.experimental import pallas as pl
from jax.experimental.pallas import tpu as pltpu

F32 = jnp.float32
BF16 = jnp.bfloat16

D_MODEL = 1024
GLA_HEADS = 4
GLA_DK = 64
GLA_DV = 128
GLA_GATE_RANK = 16
GLA_TAU = 16.0
MLA_HEADS = 4
MLA_NOPE = 128
MLA_ROPE = 64
MLA_V = 128
MLA_Q_RANK = 256
MLA_KV_RANK = 128
ROPE_THETA = 10000.0
D_GLA_QK = GLA_HEADS * GLA_DK
D_GLA = GLA_HEADS * GLA_DV
D_MLA = MLA_HEADS * MLA_V
D_FF = 4 * D_MODEL
N_MOD = 6
RMS_EPS = 1e-6
D_IN_PADDED = 2048
MLA_QK_PAD = 256
VT_ROWS = MLA_V + 16
LOG2_E = 1.4426950408889634

LANES = 128
VMEM_LIMIT_BYTES = 56 * 1024 * 1024

PRE_TM = 512
POST_TM = 512
FF_CHUNK = 1024
GLA_CHUNK = 128
GLA_BLOCK = 512
ADA_TN = 1536
ROPE_ROWS = 2048
NEG_BIG = -1e30


def _dot(a, b):
    return jnp.dot(a, b, preferred_element_type=F32)


def _dot_nt(a, b):
    return lax.dot_general(a, b, (((1,), (1,)), ((), ())), preferred_element_type=F32)


def _dot_tn(a, b):
    return lax.dot_general(a, b, (((0,), (0,)), ((), ())), preferred_element_type=F32)


def _split2(a):
    hi = a.astype(BF16)
    lo = (a - hi.astype(F32)).astype(BF16)
    return hi, lo


def _dot_f32(a, b):
    ah, al = _split2(a)
    bh, bl = _split2(b)
    return _dot(ah, bh) + _dot(ah, bl) + _dot(al, bh)


def _rms(x):
    return x * lax.rsqrt(jnp.mean(x * x, axis=-1, keepdims=True) + RMS_EPS)


def _const_spec(shape):
    zeros = (0,) * len(shape)
    return pl.BlockSpec(shape, lambda *_: zeros, pipeline_mode=pl.Buffered(1))


def _params(*semantics):
    return pltpu.CompilerParams(dimension_semantics=semantics,
                                vmem_limit_bytes=VMEM_LIMIT_BYTES)


def _adaln_kernel(c_ref, w_ref, b_ref, o_ref):
    c = c_ref[...]
    cond = c * jax.nn.sigmoid(c)
    o_ref[0] = _dot_f32(cond, w_ref[0]) + b_ref[0]


def _adaln(c_pad, w_ada, b_ada):
    depth, d, n = w_ada.shape
    rows = c_pad.shape[0]
    return pl.pallas_call(
        _adaln_kernel,
        grid=(depth, n // ADA_TN),
        in_specs=[
            pl.BlockSpec((rows, d), lambda l, j: (0, 0)),
            pl.BlockSpec((1, d, ADA_TN), lambda l, j: (l, 0, j)),
            pl.BlockSpec((1, 1, ADA_TN), lambda l, j: (l, 0, j)),
        ],
        out_specs=pl.BlockSpec((1, rows, ADA_TN), lambda l, j: (l, 0, j)),
        out_shape=jax.ShapeDtypeStruct((depth, rows, n), F32),
        compiler_params=_params("arbitrary", "arbitrary"),
        name="adaln",
    )(c_pad, w_ada, b_ada.reshape(depth, 1, n))


def _rope_kernel(pos_ref, inv_ref, cos_ref, sin_ref):
    ang = pos_ref[...].astype(F32) * inv_ref[...]
    cos_ref[...] = jnp.cos(ang)
    sin_ref[...] = jnp.sin(ang)


def _rope_tables(positions):
    t = positions.size
    half = MLA_ROPE // 2
    per_row = LANES // half
    rows = t // per_row
    inv_freq = ROPE_THETA ** (-jnp.arange(0, MLA_ROPE, 2, dtype=F32) / MLA_ROPE)
    pos_rep = jnp.repeat(positions.reshape(rows, per_row), half, axis=1)
    inv_rep = jnp.tile(inv_freq, per_row).reshape(1, LANES)
    tr = min(ROPE_ROWS, rows)
    cos, sin = pl.pallas_call(
        _rope_kernel,
        grid=(rows // tr,),
        in_specs=[pl.BlockSpec((tr, LANES), lambda i: (i, 0)),
                  pl.BlockSpec((1, LANES), lambda i: (0, 0))],
        out_specs=[pl.BlockSpec((tr, LANES), lambda i: (i, 0))] * 2,
        out_shape=[jax.ShapeDtypeStruct((rows, LANES), F32)] * 2,
        compiler_params=_params("arbitrary"),
        name="rope_tables",
    )(pos_rep, inv_rep)
    cos = jnp.tile(cos.reshape(t, half), (1, per_row))
    sin = jnp.tile(sin.reshape(t, half), (1, per_row))
    return cos, sin


def _pre_kernel(x_ref, mod_ref, cos_ref, sin_ref, w_in_ref, wg_ref, bg_ref,
                qan_ref, kvan_ref, qnn_ref, knn_ref, qnr_ref, knr_ref, wq_ref, wkv_ref,
                gq_ref, gk_ref, gv_ref, go_ref, la_ref, q_ref, k_ref, vt_ref):
    mod = mod_ref[0]
    h = _rms(x_ref[...]) * (1.0 + mod[1:2, :]) + mod[0:1, :]
    proj = _dot(h.astype(BF16), w_in_ref[...])

    gq_ref[...] = proj[:, 0:256] * (GLA_DK ** -0.5)
    gk_ref[...] = proj[:, 256:512]
    gv_ref[...] = proj[:, 512:1024].astype(BF16)
    go_ref[...] = proj[:, 1024:1536].astype(BF16)
    tail = proj[:, 1920:2048]
    z = _dot_f32(tail, wg_ref[...]) + bg_ref[...]
    log_sig = jnp.minimum(z, 0.0) - jnp.log1p(jnp.exp(-jnp.abs(z)))
    la_ref[...] = log_sig * (1.0 / GLA_TAU)

    lane = lax.broadcasted_iota(jnp.int32, (1, LANES), 1)
    low = lane < MLA_ROPE
    first_half = (lane % MLA_ROPE) < (MLA_ROPE // 2)
    cos = cos_ref[...]
    sin = jnp.where(first_half, -sin_ref[...], sin_ref[...])
    scale = (MLA_NOPE + MLA_ROPE) ** -0.5 * LOG2_E

    def rope(y):
        rot = jnp.where(first_half, pltpu.roll(y, LANES - MLA_ROPE // 2, 1),
                        pltpu.roll(y, MLA_ROPE // 2, 1))
        return y * cos + rot * sin

    def pair_rms(y):
        sq = y * y
        s_lo = jnp.sum(jnp.where(low, sq, 0.0), axis=-1, keepdims=True)
        s_hi = jnp.sum(jnp.where(low, 0.0, sq), axis=-1, keepdims=True)
        ms = jnp.where(low, s_lo, s_hi) * (1.0 / MLA_ROPE)
        return y * lax.rsqrt(ms + RMS_EPS)

    mq = _rms(proj[:, 1536:1792]) * qan_ref[...]
    qh = _dot(mq.astype(BF16), wq_ref[...])
    q_rope = []
    for p in range(MLA_HEADS // 2):
        blk = qh[:, 512 + LANES * p:512 + LANES * (p + 1)]
        q_rope.append(rope(pair_rms(blk) * qnr_ref[...]) * scale)
    for hd in range(MLA_HEADS):
        qn = _rms(qh[:, LANES * hd:LANES * (hd + 1)]) * qnn_ref[...] * scale
        q_ref[:, MLA_QK_PAD * hd:MLA_QK_PAD * hd + LANES] = qn.astype(BF16)
        keep = low if hd % 2 == 0 else jnp.logical_not(low)
        qr = jnp.where(keep, q_rope[hd // 2], 0.0)
        q_ref[:, MLA_QK_PAD * hd + LANES:MLA_QK_PAD * (hd + 1)] = qr.astype(BF16)

    mkv = _rms(proj[:, 1792:1920]) * kvan_ref[...]
    kv = _dot(mkv.astype(BF16), wkv_ref[...])
    vt = kv[:, 512:1024].T
    ones_rows = jnp.where(
        lax.broadcasted_iota(jnp.int32, (VT_ROWS - MLA_V, vt.shape[1]), 0) == 0, 1.0, 0.0)
    for hd in range(MLA_HEADS):
        vt_ref[0, VT_ROWS * hd:VT_ROWS * hd + MLA_V, :] = vt[MLA_V * hd:MLA_V * (hd + 1)].astype(BF16)
        vt_ref[0, VT_ROWS * hd + MLA_V:VT_ROWS * (hd + 1), :] = ones_rows.astype(BF16)
    sq = tail * tail
    ms = jnp.sum(jnp.where(low, sq, 0.0), axis=-1, keepdims=True) * (1.0 / MLA_ROPE)
    kr_even = rope(tail * lax.rsqrt(ms + RMS_EPS) * knr_ref[...])
    kr_odd = pltpu.roll(kr_even, MLA_ROPE, 1)
    for hd in range(MLA_HEADS):
        kn = _rms(kv[:, LANES * hd:LANES * (hd + 1)]) * knn_ref[...]
        k_ref[:, MLA_QK_PAD * hd:MLA_QK_PAD * hd + LANES] = kn.astype(BF16)
        kr = kr_even if hd % 2 == 0 else kr_odd
        k_ref[:, MLA_QK_PAD * hd + LANES:MLA_QK_PAD * (hd + 1)] = kr.astype(BF16)


def _pre_mixer(x2, mod, cos, sin, lw, seq):
    t = x2.shape[0]
    tm = min(PRE_TM, seq)
    per_batch = seq // tm
    row = lambda i: (i, 0)
    out_widths = [(D_GLA_QK, F32), (D_GLA_QK, F32), (D_GLA, BF16), (D_GLA, BF16), (D_GLA_QK, F32),
                  (MLA_HEADS * MLA_QK_PAD, BF16), (MLA_HEADS * MLA_QK_PAD, BF16)]
    vt_rows = MLA_HEADS * VT_ROWS
    return pl.pallas_call(
        _pre_kernel,
        grid=(t // tm,),
        in_specs=[
            pl.BlockSpec((tm, D_MODEL), row),
            pl.BlockSpec((1, N_MOD, D_MODEL), lambda i: (i // per_batch, 0, 0)),
            pl.BlockSpec((tm, LANES), row),
            pl.BlockSpec((tm, LANES), row),
            _const_spec((D_MODEL, D_IN_PADDED)),
            _const_spec((LANES, D_GLA_QK)),
            _const_spec((1, D_GLA_QK)),
            _const_spec((1, MLA_Q_RANK)),
            _const_spec((1, MLA_KV_RANK)),
            _const_spec((1, MLA_NOPE)),
            _const_spec((1, MLA_NOPE)),
            _const_spec((1, LANES)),
            _const_spec((1, LANES)),
            _const_spec((MLA_Q_RANK, MLA_HEADS * (MLA_NOPE + MLA_ROPE))),
            _const_spec((MLA_KV_RANK, MLA_HEADS * (MLA_NOPE + MLA_V))),
        ],
        out_specs=[pl.BlockSpec((tm, w), row) for w, _ in out_widths]
        + [pl.BlockSpec((1, vt_rows, tm), lambda i: (i, 0, 0))],
        out_shape=[jax.ShapeDtypeStruct((t, w), dt) for w, dt in out_widths]
        + [jax.ShapeDtypeStruct((t // tm, vt_rows, tm), BF16)],
        compiler_params=_params("arbitrary"),
        name="pre_mixer",
    )(x2, mod, cos, sin, lw["w_in"], lw["wg"], lw["bg"], lw["qan"], lw["kvan"], lw["qnn"],
      lw["knn"], lw["qnr"], lw["knr"], lw["wq"], lw["wkv"])


def _gla_constants(c):
    t = np.arange(c)[:, None]
    u = np.arange(c)[None, :]
    sums = []
    masks = [t == u]
    s = 1
    while s < c:
        same = (t // s) == (u // s)
        odd = ((t // s) % 2) == 1
        sums.append(np.where(odd, same & (u <= t), same & (u > t)))
        masks.append(odd & ((u // s) == (t // s) - 1))
        s *= 2
    sums.append(u <= t)
    masks = np.stack(masks)
    return (jnp.asarray(np.concatenate(sums, 0), BF16),
            jnp.asarray(np.concatenate([masks, masks], axis=-1), F32))


def _gla_kernel(q_ref, k_ref, v_ref, go_ref, la_ref, msum_ref, mask_ref, gain_ref,
                o_ref, state_ref, *, chunk, n_chunks, blocks_per_seq):
    n_levels = mask_ref.shape[0]
    pairs = GLA_HEADS // 2

    @pl.when(pl.program_id(0) % blocks_per_seq == 0)
    def _():
        state_ref[...] = jnp.zeros_like(state_ref)

    low = lax.broadcasted_iota(jnp.int32, (1, LANES), 1) < GLA_DK
    first_v = lax.broadcasted_iota(jnp.int32, (1, 2 * GLA_DV), 1) < GLA_DV
    srow = lax.broadcasted_iota(jnp.int32, (2 * GLA_DV, LANES), 0) // GLA_DV
    scol = lax.broadcasted_iota(jnp.int32, (2 * GLA_DV, LANES), 1) // GLA_DK
    diag_blocks = srow == scol

    def one_chunk(ci, carry):
        rows = pl.ds(pl.multiple_of(ci * chunk, chunk), chunk)
        dsum = _dot(msum_ref[...], la_ref[rows, :].astype(BF16))
        b = dsum[(n_levels - 1) * chunk:n_levels * chunk]
        b_last = b[chunk - 1:chunk, :]
        q = q_ref[rows, :]
        k = k_ref[rows, :]
        q_dec = (q * jnp.exp(b)).astype(BF16)
        k_dec = (k * jnp.exp(b_last - b)).astype(BF16)
        decay = jnp.exp(b_last)

        for p in range(pairs):
            cols = slice(LANES * p, LANES * (p + 1))
            q_p = q[:, cols]
            k_p = k[:, cols]
            att = jnp.zeros((chunk, 2 * chunk), F32)
            for lv in range(n_levels):
                if lv == 0:
                    qs, ke = q_p.astype(BF16), k_p.astype(BF16)
                else:
                    e = jnp.exp(dsum[(lv - 1) * chunk:lv * chunk, cols])
                    qs, ke = (q_p * e).astype(BF16), (k_p * e).astype(BF16)
                zero = jnp.zeros_like(ke)
                ks = jnp.concatenate([jnp.where(low, ke, zero), jnp.where(low, zero, ke)], axis=0)
                att = att + _dot_nt(qs, ks) * mask_ref[lv]

            st = state_ref[p]
            v_p = v_ref[rows, 2 * GLA_DV * p:2 * GLA_DV * (p + 1)]
            vz = jnp.zeros_like(v_p)
            v_bd = jnp.concatenate([jnp.where(first_v, v_p, vz), jnp.where(first_v, vz, v_p)], axis=0)
            o_pair = _dot_nt(q_dec[:, cols], st.astype(BF16)) + _dot(att.astype(BF16), v_bd)
            kv = _dot_tn(v_p, k_dec[:, cols])
            state_ref[p] = jnp.where(diag_blocks, decay[:, cols] * st + kv, 0.0)

            for hh in range(2):
                hc = slice(GLA_DV * (2 * p + hh), GLA_DV * (2 * p + hh + 1))
                g = go_ref[rows, hc].astype(F32)
                o = _rms(o_pair[:, GLA_DV * hh:GLA_DV * (hh + 1)]) * gain_ref[...]
                o_ref[rows, hc] = (o * (g * jax.nn.sigmoid(g))).astype(BF16)
        return carry

    lax.fori_loop(0, n_chunks, one_chunk, 0)


def _gla(gq, gk, gv, go, la, gain, seq):
    t = gq.shape[0]
    chunk = min(GLA_CHUNK, seq)
    tb = min(GLA_BLOCK, seq)
    msum, masks = _gla_constants(chunk)
    row = lambda i: (i, 0)
    kern = functools.partial(_gla_kernel, chunk=chunk, n_chunks=tb // chunk,
                             blocks_per_seq=seq // tb)
    return pl.pallas_call(
        kern,
        grid=(t // tb,),
        in_specs=[
            pl.BlockSpec((tb, D_GLA_QK), row),
            pl.BlockSpec((tb, D_GLA_QK), row),
            pl.BlockSpec((tb, D_GLA), row),
            pl.BlockSpec((tb, D_GLA), row),
            pl.BlockSpec((tb, D_GLA_QK), row),
            _const_spec(msum.shape),
            _const_spec(masks.shape),
            _const_spec((1, GLA_DV)),
        ],
        out_specs=pl.BlockSpec((tb, D_GLA), row),
        out_shape=jax.ShapeDtypeStruct((t, D_GLA), BF16),
        scratch_shapes=[pltpu.VMEM((GLA_HEADS // 2, 2 * GLA_DV, LANES), F32)],
        compiler_params=_params("arbitrary"),
        name="gla_scan",
    )(gq, gk, gv, go, la, msum, masks, gain)


def _attn_kernel(q_ref, k_ref, vt_ref, o_ref, s_ref, s2_ref, m_ref, acc_ref, *, tq):
    qi = pl.program_id(2)
    q = q_ref[...]
    m_ref[...] = jnp.full_like(m_ref, NEG_BIG)
    acc_ref[...] = jnp.zeros_like(acc_ref)

    def scores(j):
        rows = pl.ds(pl.multiple_of(j * tq, tq), tq)
        return _dot_nt(k_ref[rows, :], q)

    def softmax_update(j, s):
        m_prev = m_ref[...]
        m_new = jnp.maximum(m_prev, jnp.max(s, axis=0, keepdims=True))
        alpha = jnp.exp2(m_prev - m_new)
        p = jnp.exp2((s - m_new).astype(BF16))
        m_ref[...] = m_new
        return alpha, p

    def accumulate(j, alpha, p):
        acc_ref[...] = alpha * acc_ref[...] + _dot(vt_ref[j], p)

    def advance(j, cur_ref, nxt_ref):
        nxt_ref[...] = scores(j + 1)
        alpha, p = softmax_update(j, cur_ref[...])
        accumulate(j, alpha, p)

    def finish(cur_ref):
        kr = lax.broadcasted_iota(jnp.int32, (tq, tq), 0)
        qc = lax.broadcasted_iota(jnp.int32, (tq, tq), 1)
        alpha, p = softmax_update(qi, jnp.where(kr <= qc, cur_ref[...], NEG_BIG))
        accumulate(qi, alpha, p)

    s_ref[...] = scores(0)

    def body(jj, carry):
        advance(2 * jj, s_ref, s2_ref)
        advance(2 * jj + 1, s2_ref, s_ref)
        return carry

    lax.fori_loop(0, qi // 2, body, 0)

    @pl.when(qi % 2 == 0)
    def _():
        finish(s_ref)

    @pl.when(qi % 2 == 1)
    def _():
        advance(qi - 1, s_ref, s2_ref)
        finish(s2_ref)

    out_t = acc_ref[0:MLA_V, :] * (1.0 / acc_ref[MLA_V:MLA_V + 1, :])
    o_ref[...] = out_t.T.astype(BF16)


def _mla_attention(q, k, vt, batch, seq):
    t = q.shape[0]
    tq = vt.shape[2]
    nq = seq // tq
    return pl.pallas_call(
        functools.partial(_attn_kernel, tq=tq),
        grid=(batch, MLA_HEADS, nq),
        in_specs=[
            pl.BlockSpec((tq, MLA_QK_PAD), lambda b, h, i: (b * nq + i, h)),
            pl.BlockSpec((seq, MLA_QK_PAD), lambda b, h, i: (b, h)),
            pl.BlockSpec((nq, VT_ROWS, tq), lambda b, h, i: (b, h, 0)),
        ],
        out_specs=pl.BlockSpec((tq, MLA_V), lambda b, h, i: (b * nq + i, h)),
        out_shape=jax.ShapeDtypeStruct((t, D_MLA), BF16),
        scratch_shapes=[pltpu.VMEM((tq, tq), F32), pltpu.VMEM((tq, tq), F32),
                        pltpu.VMEM((1, tq), F32),
                        pltpu.VMEM((VT_ROWS, tq), F32)],
        compiler_params=_params("arbitrary", "arbitrary", "arbitrary"),
        name="mla_attention",
    )(q, k, vt)


def _post_kernel(x_ref, gla_ref, mla_ref, mod_ref, wo_ref, w1_ref, w2_ref, o_ref):
    mod = mod_ref[0]
    mix = _dot(gla_ref[...], wo_ref[0:D_GLA, :]) + _dot(mla_ref[...], wo_ref[D_GLA:, :])
    x1 = x_ref[...] + mod[2:3, :] * mix
    h = (_rms(x1) * (1.0 + mod[4:5, :]) + mod[3:4, :]).astype(BF16)
    acc = jnp.zeros_like(x1)
    for c in range(D_FF // FF_CHUNK):
        cols = slice(FF_CHUNK * c, FF_CHUNK * (c + 1))
        u = jnp.maximum(_dot(h, w1_ref[:, cols]), 0.0)
        acc = acc + _dot((u * u).astype(BF16), w2_ref[cols, :])
    o_ref[...] = x1 + mod[5:6, :] * acc


def _post_mixer(x2, gla, mla, mod, lw, seq):
    t = x2.shape[0]
    tm = min(POST_TM, seq)
    per_batch = seq // tm
    row = lambda i: (i, 0)
    return pl.pallas_call(
        _post_kernel,
        grid=(t // tm,),
        in_specs=[
            pl.BlockSpec((tm, D_MODEL), row),
            pl.BlockSpec((tm, D_GLA), row),
            pl.BlockSpec((tm, D_MLA), row),
            pl.BlockSpec((1, N_MOD, D_MODEL), lambda i: (i // per_batch, 0, 0)),
            _const_spec((D_GLA + D_MLA, D_MODEL)),
            _const_spec((D_MODEL, D_FF)),
            _const_spec((D_FF, D_MODEL)),
        ],
        out_specs=pl.BlockSpec((tm, D_MODEL), row),
        out_shape=jax.ShapeDtypeStruct((t, D_MODEL), F32),
        compiler_params=_params("arbitrary"),
        name="post_mixer",
    )(x2, gla, mla, mod, lw["wo"], lw["w1"], lw["w2"])


def _layer_weights(l, w_in, w_gate_up, b_gate, gla_out_norm, q_a_norm, w_q_up, kv_a_norm,
                   w_kv_up, q_norm_nope, k_norm_nope, q_norm_rope, k_norm_rope, w_out,
                   w_mlp_up, w_mlp_down):
    wi = w_in[l]
    w_in_p = jnp.concatenate(
        [wi[:, 0:1536], wi[:, 1552:1808], wi[:, 1808:1936], wi[:, 1936:2000], wi[:, 1536:1552],
         jnp.zeros((D_MODEL, D_IN_PADDED - 2000), F32)], axis=1).astype(BF16)
    wg = jnp.zeros((LANES, D_GLA_QK), F32).at[MLA_ROPE:MLA_ROPE + GLA_GATE_RANK].set(w_gate_up[l])
    wq = w_q_up[l].reshape(MLA_Q_RANK, MLA_HEADS, MLA_NOPE + MLA_ROPE)
    wq = jnp.concatenate([wq[:, :, :MLA_NOPE].reshape(MLA_Q_RANK, -1),
                          wq[:, :, MLA_NOPE:].reshape(MLA_Q_RANK, -1)], axis=1).astype(BF16)
    wkv = w_kv_up[l].reshape(MLA_KV_RANK, MLA_HEADS, MLA_NOPE + MLA_V)
    wkv = jnp.concatenate([wkv[:, :, :MLA_NOPE].reshape(MLA_KV_RANK, -1),
                           wkv[:, :, MLA_NOPE:].reshape(MLA_KV_RANK, -1)], axis=1).astype(BF16)
    return dict(
        w_in=w_in_p, wg=wg, bg=b_gate[l].reshape(1, -1),
        gla_gain=gla_out_norm[l].reshape(1, -1),
        qan=q_a_norm[l].reshape(1, -1), kvan=kv_a_norm[l].reshape(1, -1),
        qnn=q_norm_nope[l].reshape(1, -1), knn=k_norm_nope[l].reshape(1, -1),
        qnr=jnp.tile(q_norm_rope[l], 2).reshape(1, -1),
        knr=jnp.concatenate([k_norm_rope[l], jnp.zeros((LANES - MLA_ROPE,), F32)]).reshape(1, -1),
        wq=wq, wkv=wkv, wo=w_out[l].astype(BF16),
        w1=w_mlp_up[l].astype(BF16), w2=w_mlp_down[l].astype(BF16))


def kernel(x, c, positions, w_ada, b_ada, w_in, w_gate_up, b_gate, gla_out_norm, q_a_norm, w_q_up, kv_a_norm, w_kv_up, q_norm_nope, k_norm_nope, q_norm_rope, k_norm_rope, w_out, w_mlp_up, w_mlp_down):
    batch, seq, d = x.shape
    depth = w_ada.shape[0]
    cos, sin = _rope_tables(positions)
    c_pad = jnp.concatenate([c, jnp.zeros((8 - batch % 8 if batch % 8 else 0, d), F32)], axis=0)
    mods = _adaln(c_pad, w_ada, b_ada)[:, :batch].reshape(depth, batch, N_MOD, d)
    x2 = x.reshape(batch * seq, d)
    for l in range(depth):
        lw = _layer_weights(l, w_in, w_gate_up, b_gate, gla_out_norm, q_a_norm, w_q_up,
                            kv_a_norm, w_kv_up, q_norm_nope, k_norm_nope, q_norm_rope,
                            k_norm_rope, w_out, w_mlp_up, w_mlp_down)
        gq, gk, gv, go, la, q, k, vt = _pre_mixer(x2, mods[l], cos, sin, lw, seq)
        gla = _gla(gq, gk, gv, go, la, lw["gla_gain"], seq)
        mla = _mla_attention(q, k, vt, batch, seq)
        x2 = _post_mixer(x2, gla, mla, mods[l], lw, seq)
    return x2.reshape(batch, seq, d)
```

```python
import functools

import numpy as np
import jax
import jax.numpy as jnp
from jax import lax
from jax.experimental import pallas as pl
from jax.experimental.pallas import tpu as pltpu

F32 = jnp.float32
BF16 = jnp.bfloat16

D_MODEL = 1024
GLA_HEADS = 4
GLA_DK = 64
GLA_DV = 128
GLA_GATE_RANK = 16
GLA_TAU = 16.0
MLA_HEADS = 4
MLA_NOPE = 128
MLA_ROPE = 64
MLA_V = 128
MLA_Q_RANK = 256
MLA_KV_RANK = 128
ROPE_THETA = 10000.0
D_GLA_QK = GLA_HEADS * GLA_DK
D_GLA = GLA_HEADS * GLA_DV
D_MLA = MLA_HEADS * MLA_V
D_FF = 4 * D_MODEL
N_MOD = 6
RMS_EPS = 1e-6
D_IN_PADDED = 2048
MLA_QK_PAD = 256
VT_ROWS = MLA_V + 16
LOG2_E = 1.4426950408889634

LANES = 128
VMEM_LIMIT_BYTES = 56 * 1024 * 1024

PRE_TM = 1024
PRE_SLABS = 4
KV_TILE = 512
ATTN_TQ = 1024
POST_TM = 512
FF_CHUNK = 1024
GLA_CHUNK = 128
GLA_BLOCK = 512
ADA_TN = 1536
ROPE_ROWS = 2048
NEG_BIG = -1e30


def _dot(a, b):
    return jnp.dot(a, b, preferred_element_type=F32)


def _dot_nt(a, b):
    return lax.dot_general(a, b, (((1,), (1,)), ((), ())), preferred_element_type=F32)


def _dot_tn(a, b):
    return lax.dot_general(a, b, (((0,), (0,)), ((), ())), preferred_element_type=F32)


def _split2(a):
    hi = a.astype(BF16)
    lo = (a - hi.astype(F32)).astype(BF16)
    return hi, lo


def _dot_f32(a, b):
    ah, al = _split2(a)
    bh, bl = _split2(b)
    return _dot(ah, bh) + _dot(ah, bl) + _dot(al, bh)


def _rms(x):
    return x * lax.rsqrt(jnp.mean(x * x, axis=-1, keepdims=True) + RMS_EPS)


def _const_spec(shape):
    zeros = (0,) * len(shape)
    return pl.BlockSpec(shape, lambda *_: zeros, pipeline_mode=pl.Buffered(1))


def _params(*semantics):
    return pltpu.CompilerParams(dimension_semantics=semantics,
                                vmem_limit_bytes=VMEM_LIMIT_BYTES)


def _adaln_kernel(c_ref, w_ref, b_ref, o_ref):
    c = c_ref[...]
    cond = c * jax.nn.sigmoid(c)
    o_ref[0] = _dot_f32(cond, w_ref[0]) + b_ref[0]


def _adaln(c_pad, w_ada, b_ada):
    depth, d, n = w_ada.shape
    rows = c_pad.shape[0]
    return pl.pallas_call(
        _adaln_kernel,
        grid=(depth, n // ADA_TN),
        in_specs=[
            pl.BlockSpec((rows, d), lambda l, j: (0, 0)),
            pl.BlockSpec((1, d, ADA_TN), lambda l, j: (l, 0, j)),
            pl.BlockSpec((1, 1, ADA_TN), lambda l, j: (l, 0, j)),
        ],
        out_specs=pl.BlockSpec((1, rows, ADA_TN), lambda l, j: (l, 0, j)),
        out_shape=jax.ShapeDtypeStruct((depth, rows, n), F32),
        compiler_params=_params("arbitrary", "arbitrary"),
        name="adaln",
    )(c_pad, w_ada, b_ada.reshape(depth, 1, n))


def _rope_kernel(pos_ref, inv_ref, cos_ref, sin_ref):
    ang = pos_ref[...].astype(F32) * inv_ref[...]
    cos_ref[...] = jnp.cos(ang)
    sin_ref[...] = jnp.sin(ang)


def _rope_tables(positions):
    t = positions.size
    half = MLA_ROPE // 2
    per_row = LANES // half
    rows = t // per_row
    inv_freq = ROPE_THETA ** (-jnp.arange(0, MLA_ROPE, 2, dtype=F32) / MLA_ROPE)
    pos_rep = jnp.repeat(positions.reshape(rows, per_row), half, axis=1)
    inv_rep = jnp.tile(inv_freq, per_row).reshape(1, LANES)
    tr = min(ROPE_ROWS, rows)
    cos, sin = pl.pallas_call(
        _rope_kernel,
        grid=(rows // tr,),
        in_specs=[pl.BlockSpec((tr, LANES), lambda i: (i, 0)),
                  pl.BlockSpec((1, LANES), lambda i: (0, 0))],
        out_specs=[pl.BlockSpec((tr, LANES), lambda i: (i, 0))] * 2,
        out_shape=[jax.ShapeDtypeStruct((rows, LANES), F32)] * 2,
        compiler_params=_params("arbitrary"),
        name="rope_tables",
    )(pos_rep, inv_rep)
    cos = jnp.tile(cos.reshape(t, half), (1, per_row))
    sin = jnp.tile(sin.reshape(t, half), (1, per_row))
    return cos, sin


def _norm_matrices():
    i = np.arange(256)[:, None]
    j = np.arange(256)[None, :]
    full = np.full((256, 256), 1.0 / 256)
    seg128 = ((i // 128) == (j // 128)) / 128.0
    seg64 = ((i // 64) == (j // 64)) / 64.0
    kvt = np.where((i < 128) & (j < 128), 1.0 / 128,
                   np.where((i >= 128) & (i < 192) & (j >= 128) & (j < 192), 1.0 / 64, 0.0))
    return jnp.asarray(np.stack([full, seg128, seg64, kvt]), BF16)


G_FULL, G_SEG128, G_SEG64, G_KVT = range(4)


def _pre_kernel(x_ref, mod_ref, cos_ref, sin_ref, w_in_ref, wg_ref, bg_ref, gmat_ref,
                qan_ref, kvt_ref, qnn_ref, knn_ref, qnr_ref, wq_ref, wkv_ref,
                gq_ref, gk_ref, gv_ref, go_ref, la_ref, q_ref, k_ref, vt_ref):
    mod = mod_ref[0]
    lane = lax.broadcasted_iota(jnp.int32, (1, LANES), 1)
    low = lane < MLA_ROPE
    first_half = (lane % MLA_ROPE) < (MLA_ROPE // 2)
    scale = (MLA_NOPE + MLA_ROPE) ** -0.5 * LOG2_E
    slab = x_ref.shape[0] // PRE_SLABS
    for si in range(PRE_SLABS):
        _pre_slab(slab * si, slab, mod, low, first_half, scale,
                  x_ref, cos_ref, sin_ref, w_in_ref, wg_ref, bg_ref, gmat_ref, qan_ref, kvt_ref,
                  qnn_ref, knn_ref, qnr_ref, wq_ref, wkv_ref, gq_ref, gk_ref, gv_ref, go_ref,
                  la_ref, q_ref, k_ref, vt_ref)


def _pre_slab(r0, nrows, mod, low, first_half, scale, x_ref, cos_ref, sin_ref, w_in_ref, wg_ref,
              bg_ref, gmat_ref, qan_ref, kvt_ref, qnn_ref, knn_ref, qnr_ref, wq_ref, wkv_ref,
              gq_ref, gk_ref, gv_ref, go_ref, la_ref, q_ref, k_ref, vt_ref):
    rs = slice(r0, r0 + nrows)
    h = _rms(x_ref[rs, :]) * (1.0 + mod[1:2, :]) + mod[0:1, :]
    proj = _dot(h.astype(BF16), w_in_ref[...])

    def seg_rms(y, g):
        outs = []
        for c in range(y.shape[1] // 256):
            blk = y[:, 256 * c:256 * (c + 1)]
            ms = _dot((blk * blk).astype(BF16), gmat_ref[g])
            outs.append(blk * lax.rsqrt(ms + RMS_EPS))
        return outs[0] if len(outs) == 1 else jnp.concatenate(outs, axis=1)

    gq_ref[rs, :] = proj[:, 0:256] * (GLA_DK ** -0.5)
    gk_ref[rs, :] = proj[:, 256:512]
    gv_ref[rs, :] = proj[:, 512:1024].astype(BF16)
    go_ref[rs, :] = proj[:, 1024:1536].astype(BF16)
    tail = proj[:, 1920:2048]
    z = _dot_f32(tail, wg_ref[...]) + bg_ref[...]
    log_sig = jnp.minimum(z, 0.0) - jnp.log1p(jnp.exp(-jnp.abs(z)))
    la_ref[rs, :] = log_sig * (1.0 / GLA_TAU)

    cos = cos_ref[rs, :]
    sin = jnp.where(first_half, -sin_ref[rs, :], sin_ref[rs, :])

    def rope(y):
        rot = jnp.where(first_half, pltpu.roll(y, LANES - MLA_ROPE // 2, 1),
                        pltpu.roll(y, MLA_ROPE // 2, 1))
        return y * cos + rot * sin

    mq = seg_rms(proj[:, 1536:1792], G_FULL) * qan_ref[...]
    qh = _dot(mq.astype(BF16), wq_ref[...])
    qn = seg_rms(qh[:, 0:512], G_SEG128) * (qnn_ref[...] * scale)
    qr = seg_rms(qh[:, 512:768], G_SEG64) * qnr_ref[...]
    q_rope = [rope(qr[:, LANES * p:LANES * (p + 1)]) * scale for p in range(MLA_HEADS // 2)]
    for hd in range(MLA_HEADS):
        q_ref[rs, MLA_QK_PAD * hd:MLA_QK_PAD * hd + LANES] = qn[:, LANES * hd:LANES * (hd + 1)].astype(BF16)
        keep = low if hd % 2 == 0 else jnp.logical_not(low)
        q_ref[rs, MLA_QK_PAD * hd + LANES:MLA_QK_PAD * (hd + 1)] = jnp.where(
            keep, q_rope[hd // 2], 0.0).astype(BF16)

    kvt = seg_rms(proj[:, 1792:2048], G_KVT) * kvt_ref[...]
    kv = _dot(kvt[:, 0:MLA_KV_RANK].astype(BF16), wkv_ref[...])
    vt = kv[:, 512:1024].T
    ones_rows = jnp.where(
        lax.broadcasted_iota(jnp.int32, (VT_ROWS - MLA_V, nrows), 0) == 0, 1.0, 0.0)
    tile, c0 = r0 // KV_TILE, r0 % KV_TILE
    for hd in range(MLA_HEADS):
        vt_ref[tile, VT_ROWS * hd:VT_ROWS * hd + MLA_V, c0:c0 + nrows] = (
            vt[MLA_V * hd:MLA_V * (hd + 1)].astype(BF16))
        vt_ref[tile, VT_ROWS * hd + MLA_V:VT_ROWS * (hd + 1), c0:c0 + nrows] = ones_rows.astype(BF16)
    kn = seg_rms(kv[:, 0:512], G_SEG128) * knn_ref[...]
    kr_even = rope(kvt[:, MLA_KV_RANK:])
    kr_odd = pltpu.roll(kr_even, MLA_ROPE, 1)
    for hd in range(MLA_HEADS):
        k_ref[rs, MLA_QK_PAD * hd:MLA_QK_PAD * hd + LANES] = kn[:, LANES * hd:LANES * (hd + 1)].astype(BF16)
        kr = kr_even if hd % 2 == 0 else kr_odd
        k_ref[rs, MLA_QK_PAD * hd + LANES:MLA_QK_PAD * (hd + 1)] = kr.astype(BF16)


def _pre_mixer(x2, mod, cos, sin, lw, seq):
    t = x2.shape[0]
    tm = min(PRE_TM, seq)
    per_batch = seq // tm
    kv_tile = min(KV_TILE, seq)
    row = lambda i: (i, 0)
    out_widths = [(D_GLA_QK, F32), (D_GLA_QK, F32), (D_GLA, BF16), (D_GLA, BF16), (D_GLA_QK, F32),
                  (MLA_HEADS * MLA_QK_PAD, BF16), (MLA_HEADS * MLA_QK_PAD, BF16)]
    vt_rows = MLA_HEADS * VT_ROWS
    return pl.pallas_call(
        _pre_kernel,
        grid=(t // tm,),
        in_specs=[
            pl.BlockSpec((tm, D_MODEL), row),
            pl.BlockSpec((1, N_MOD, D_MODEL), lambda i: (i // per_batch, 0, 0)),
            pl.BlockSpec((tm, LANES), row),
            pl.BlockSpec((tm, LANES), row),
            _const_spec((D_MODEL, D_IN_PADDED)),
            _const_spec((LANES, D_GLA_QK)),
            _const_spec((1, D_GLA_QK)),
            _const_spec((4, 256, 256)),
            _const_spec((1, MLA_Q_RANK)),
            _const_spec((1, 256)),
            _const_spec((1, MLA_HEADS * MLA_NOPE)),
            _const_spec((1, MLA_HEADS * MLA_NOPE)),
            _const_spec((1, MLA_HEADS * MLA_ROPE)),
            _const_spec((MLA_Q_RANK, MLA_HEADS * (MLA_NOPE + MLA_ROPE))),
            _const_spec((MLA_KV_RANK, MLA_HEADS * (MLA_NOPE + MLA_V))),
        ],
        out_specs=[pl.BlockSpec((tm, w), row) for w, _ in out_widths]
        + [pl.BlockSpec((tm // kv_tile, vt_rows, kv_tile), lambda i: (i, 0, 0))],
        out_shape=[jax.ShapeDtypeStruct((t, w), dt) for w, dt in out_widths]
        + [jax.ShapeDtypeStruct((t // kv_tile, vt_rows, kv_tile), BF16)],
        compiler_params=_params("arbitrary"),
        name="pre_mixer",
    )(x2, mod, cos, sin, lw["w_in"], lw["wg"], lw["bg"], _norm_matrices(), lw["qan"], lw["kvt"],
      lw["qnn"], lw["knn"], lw["qnr"], lw["wq"], lw["wkv"])


def _gla_constants(c):
    t = np.arange(c)[:, None]
    u = np.arange(c)[None, :]
    sums = []
    masks = [t == u]
    s = 1
    while s < c:
        same = (t // s) == (u // s)
        odd = ((t // s) % 2) == 1
        sums.append(np.where(odd, same & (u <= t), same & (u > t)))
        masks.append(odd & ((u // s) == (t // s) - 1))
        s *= 2
    sums.append(u <= t)
    masks = np.stack(masks)
    return (jnp.asarray(np.concatenate(sums, 0), BF16),
            jnp.asarray(np.concatenate([masks, masks], axis=-1), F32))


def _gla_kernel(q_ref, k_ref, v_ref, go_ref, la_ref, msum_ref, mask_ref, gain_ref,
                o_ref, state_ref, *, chunk, n_chunks, blocks_per_seq):
    n_levels = mask_ref.shape[0]
    pairs = GLA_HEADS // 2

    @pl.when(pl.program_id(0) % blocks_per_seq == 0)
    def _():
        state_ref[...] = jnp.zeros_like(state_ref)

    low = lax.broadcasted_iota(jnp.int32, (1, LANES), 1) < GLA_DK
    first_v = lax.broadcasted_iota(jnp.int32, (1, 2 * GLA_DV), 1) < GLA_DV
    srow = lax.broadcasted_iota(jnp.int32, (2 * GLA_DV, LANES), 0) // GLA_DV
    scol = lax.broadcasted_iota(jnp.int32, (2 * GLA_DV, LANES), 1) // GLA_DK
    diag_blocks = srow == scol

    def one_chunk(ci, carry):
        rows = pl.ds(pl.multiple_of(ci * chunk, chunk), chunk)
        dsum = _dot(msum_ref[...], la_ref[rows, :].astype(BF16))
        b = dsum[(n_levels - 1) * chunk:n_levels * chunk]
        b_last = b[chunk - 1:chunk, :]
        q = q_ref[rows, :]
        k = k_ref[rows, :]
        q_dec = (q * jnp.exp(b)).astype(BF16)
        k_dec = (k * jnp.exp(b_last - b)).astype(BF16)
        decay = jnp.exp(b_last)

        for p in range(pairs):
            cols = slice(LANES * p, LANES * (p + 1))
            q_p = q[:, cols]
            k_p = k[:, cols]
            att = jnp.zeros((chunk, 2 * chunk), F32)
            for lv in range(n_levels):
                if lv == 0:
                    qs, ke = q_p.astype(BF16), k_p.astype(BF16)
                else:
                    e = jnp.exp(dsum[(lv - 1) * chunk:lv * chunk, cols])
                    qs, ke = (q_p * e).astype(BF16), (k_p * e).astype(BF16)
                zero = jnp.zeros_like(ke)
                ks = jnp.concatenate([jnp.where(low, ke, zero), jnp.where(low, zero, ke)], axis=0)
                att = att + _dot_nt(qs, ks) * mask_ref[lv]

            st = state_ref[p]
            v_p = v_ref[rows, 2 * GLA_DV * p:2 * GLA_DV * (p + 1)]
            vz = jnp.zeros_like(v_p)
            v_bd = jnp.concatenate([jnp.where(first_v, v_p, vz), jnp.where(first_v, vz, v_p)], axis=0)
            o_pair = _dot_nt(q_dec[:, cols], st.astype(BF16)) + _dot(att.astype(BF16), v_bd)
            kv = _dot_tn(v_p, k_dec[:, cols])
            state_ref[p] = jnp.where(diag_blocks, decay[:, cols] * st + kv, 0.0)

            for hh in range(2):
                hc = slice(GLA_DV * (2 * p + hh), GLA_DV * (2 * p + hh + 1))
                g = go_ref[rows, hc].astype(F32)
                o = _rms(o_pair[:, GLA_DV * hh:GLA_DV * (hh + 1)]) * gain_ref[...]
                o_ref[rows, hc] = (o * (g * jax.nn.sigmoid(g))).astype(BF16)
        return carry

    lax.fori_loop(0, n_chunks, one_chunk, 0)


def _gla(gq, gk, gv, go, la, gain, seq):
    t = gq.shape[0]
    chunk = min(GLA_CHUNK, seq)
    tb = min(GLA_BLOCK, seq)
    msum, masks = _gla_constants(chunk)
    row = lambda i: (i, 0)
    kern = functools.partial(_gla_kernel, chunk=chunk, n_chunks=tb // chunk,
                             blocks_per_seq=seq // tb)
    return pl.pallas_call(
        kern,
        grid=(t // tb,),
        in_specs=[
            pl.BlockSpec((tb, D_GLA_QK), row),
            pl.BlockSpec((tb, D_GLA_QK), row),
            pl.BlockSpec((tb, D_GLA), row),
            pl.BlockSpec((tb, D_GLA), row),
            pl.BlockSpec((tb, D_GLA_QK), row),
            _const_spec(msum.shape),
            _const_spec(masks.shape),
            _const_spec((1, GLA_DV)),
        ],
        out_specs=pl.BlockSpec((tb, D_GLA), row),
        out_shape=jax.ShapeDtypeStruct((t, D_GLA), BF16),
        scratch_shapes=[pltpu.VMEM((GLA_HEADS // 2, 2 * GLA_DV, LANES), F32)],
        compiler_params=_params("arbitrary"),
        name="gla_scan",
    )(gq, gk, gv, go, la, msum, masks, gain)


def _attn_kernel(q_ref, k_ref, vt_ref, o_ref, s_ref, s2_ref, smax_ref, smax2_ref, m_ref, acc_ref,
                 *, tq, tk):
    qi = pl.program_id(2)
    ratio = tq // tk
    q_t = q_ref[...].astype(F32).T.astype(BF16)
    m_ref[...] = jnp.full_like(m_ref, NEG_BIG)
    acc_ref[...] = jnp.zeros_like(acc_ref)

    def scores(j, c0):
        rows = pl.ds(pl.multiple_of(j * tk, tk), tk)
        return _dot(k_ref[rows, :], q_t[:, c0:])

    def softmax_update(s, s_max, c0):
        m_prev = m_ref[:, c0:]
        m_new = jnp.maximum(m_prev, s_max)
        alpha = jnp.exp2(m_prev - m_new)
        p = jnp.exp2((s - m_new).astype(BF16))
        m_ref[:, c0:] = m_new
        return alpha, p

    def accumulate(j, alpha, p, c0):
        acc_ref[:, c0:] = alpha * acc_ref[:, c0:] + _dot(vt_ref[j], p)

    def produce(j, dst_ref, dst_max_ref):
        s = scores(j, 0)
        dst_ref[...] = s
        dst_max_ref[...] = jnp.max(s, axis=0, keepdims=True)

    def advance(j, cur_ref, cur_max_ref, nxt_ref, nxt_max_ref):
        produce(j + 1, nxt_ref, nxt_max_ref)
        alpha, p = softmax_update(cur_ref[...], cur_max_ref[...], 0)
        accumulate(j, alpha, p, 0)

    produce(0, s_ref, smax_ref)

    def body(jj, carry):
        advance(2 * jj, s_ref, smax_ref, s2_ref, smax2_ref)
        advance(2 * jj + 1, s2_ref, smax2_ref, s_ref, smax_ref)
        return carry

    lax.fori_loop(0, (qi * ratio) // 2, body, 0)

    bufs = (s_ref, s2_ref)
    for d in range(ratio):
        c0 = d * tk
        width = tq - c0
        cur = bufs[d % 2]
        if d + 1 < ratio:
            bufs[(d + 1) % 2][:, 0:width - tk] = scores(qi * ratio + d + 1, c0 + tk)
        kr = lax.broadcasted_iota(jnp.int32, (tk, width), 0)
        qc = lax.broadcasted_iota(jnp.int32, (tk, width), 1)
        s = jnp.where(kr <= qc, cur[:, 0:width], NEG_BIG)
        alpha, p = softmax_update(s, jnp.max(s, axis=0, keepdims=True), c0)
        accumulate(qi * ratio + d, alpha, p, c0)

    out_t = acc_ref[0:MLA_V, :] * (1.0 / acc_ref[MLA_V:MLA_V + 1, :])
    o_ref[...] = out_t.T.astype(BF16)


def _mla_attention(q, k, vt, batch, seq):
    t = q.shape[0]
    tk = vt.shape[2]
    tq = min(ATTN_TQ, seq)
    assert tq % tk == 0 and (tq // tk) % 2 == 0 or tq == seq
    nq = seq // tq
    return pl.pallas_call(
        functools.partial(_attn_kernel, tq=tq, tk=tk),
        grid=(batch, MLA_HEADS, nq),
        in_specs=[
            pl.BlockSpec((tq, MLA_QK_PAD), lambda b, h, i: (b * nq + i, h)),
            pl.BlockSpec((seq, MLA_QK_PAD), lambda b, h, i: (b, h)),
            pl.BlockSpec((seq // tk, VT_ROWS, tk), lambda b, h, i: (b, h, 0)),
        ],
        out_specs=pl.BlockSpec((tq, MLA_V), lambda b, h, i: (b * nq + i, h)),
        out_shape=jax.ShapeDtypeStruct((t, D_MLA), BF16),
        scratch_shapes=[pltpu.VMEM((tk, tq), F32), pltpu.VMEM((tk, tq), F32),
                        pltpu.VMEM((1, tq), F32), pltpu.VMEM((1, tq), F32),
                        pltpu.VMEM((1, tq), F32), pltpu.VMEM((VT_ROWS, tq), F32)],
        compiler_params=_params("arbitrary", "arbitrary", "arbitrary"),
        name="mla_attention",
    )(q, k, vt)


def _post_kernel(x_ref, gla_ref, mla_ref, mod_ref, wo_ref, w1_ref, w2_ref, o_ref):
    mod = mod_ref[0]
    mix = _dot(gla_ref[...], wo_ref[0:D_GLA, :]) + _dot(mla_ref[...], wo_ref[D_GLA:, :])
    x1 = x_ref[...] + mod[2:3, :] * mix
    h = (_rms(x1) * (1.0 + mod[4:5, :]) + mod[3:4, :]).astype(BF16)
    acc = jnp.zeros_like(x1)
    for c in range(D_FF // FF_CHUNK):
        cols = slice(FF_CHUNK * c, FF_CHUNK * (c + 1))
        u = jnp.maximum(_dot(h, w1_ref[:, cols]), 0.0)
        acc = acc + _dot((u * u).astype(BF16), w2_ref[cols, :])
    o_ref[...] = x1 + mod[5:6, :] * acc


def _post_mixer(x2, gla, mla, mod, lw, seq):
    t = x2.shape[0]
    tm = min(POST_TM, seq)
    per_batch = seq // tm
    row = lambda i: (i, 0)
    return pl.pallas_call(
        _post_kernel,
        grid=(t // tm,),
        in_specs=[
            pl.BlockSpec((tm, D_MODEL), row),
            pl.BlockSpec((tm, D_GLA), row),
            pl.BlockSpec((tm, D_MLA), row),
            pl.BlockSpec((1, N_MOD, D_MODEL), lambda i: (i // per_batch, 0, 0)),
            _const_spec((D_GLA + D_MLA, D_MODEL)),
            _const_spec((D_MODEL, D_FF)),
            _const_spec((D_FF, D_MODEL)),
        ],
        out_specs=pl.BlockSpec((tm, D_MODEL), row),
        out_shape=jax.ShapeDtypeStruct((t, D_MODEL), F32),
        compiler_params=_params("arbitrary"),
        name="post_mixer",
    )(x2, gla, mla, mod, lw["wo"], lw["w1"], lw["w2"])


def _layer_weights(l, w_in, w_gate_up, b_gate, gla_out_norm, q_a_norm, w_q_up, kv_a_norm,
                   w_kv_up, q_norm_nope, k_norm_nope, q_norm_rope, k_norm_rope, w_out,
                   w_mlp_up, w_mlp_down):
    wi = w_in[l]
    w_in_p = jnp.concatenate(
        [wi[:, 0:1536], wi[:, 1552:1808], wi[:, 1808:1936], wi[:, 1936:2000], wi[:, 1536:1552],
         jnp.zeros((D_MODEL, D_IN_PADDED - 2000), F32)], axis=1).astype(BF16)
    wg = jnp.zeros((LANES, D_GLA_QK), F32).at[MLA_ROPE:MLA_ROPE + GLA_GATE_RANK].set(w_gate_up[l])
    wq = w_q_up[l].reshape(MLA_Q_RANK, MLA_HEADS, MLA_NOPE + MLA_ROPE)
    wq = jnp.concatenate([wq[:, :, :MLA_NOPE].reshape(MLA_Q_RANK, -1),
                          wq[:, :, MLA_NOPE:].reshape(MLA_Q_RANK, -1)], axis=1).astype(BF16)
    wkv = w_kv_up[l].reshape(MLA_KV_RANK, MLA_HEADS, MLA_NOPE + MLA_V)
    wkv = jnp.concatenate([wkv[:, :, :MLA_NOPE].reshape(MLA_KV_RANK, -1),
                           wkv[:, :, MLA_NOPE:].reshape(MLA_KV_RANK, -1)], axis=1).astype(BF16)
    return dict(
        w_in=w_in_p, wg=wg, bg=b_gate[l].reshape(1, -1),
        gla_gain=gla_out_norm[l].reshape(1, -1),
        qan=q_a_norm[l].reshape(1, -1),
        kvt=jnp.concatenate([kv_a_norm[l], k_norm_rope[l],
                             jnp.zeros((LANES - MLA_ROPE,), F32)]).reshape(1, -1),
        qnn=jnp.tile(q_norm_nope[l], MLA_HEADS).reshape(1, -1),
        knn=jnp.tile(k_norm_nope[l], MLA_HEADS).reshape(1, -1),
        qnr=jnp.tile(q_norm_rope[l], MLA_HEADS).reshape(1, -1),
        wq=wq, wkv=wkv, wo=w_out[l].astype(BF16),
        w1=w_mlp_up[l].astype(BF16), w2=w_mlp_down[l].astype(BF16))


def kernel(x, c, positions, w_ada, b_ada, w_in, w_gate_up, b_gate, gla_out_norm, q_a_norm, w_q_up, kv_a_norm, w_kv_up, q_norm_nope, k_norm_nope, q_norm_rope, k_norm_rope, w_out, w_mlp_up, w_mlp_down):
    batch, seq, d = x.shape
    depth = w_ada.shape[0]
    cos, sin = _rope_tables(positions)
    c_pad = jnp.concatenate([c, jnp.zeros((8 - batch % 8 if batch % 8 else 0, d), F32)], axis=0)
    mods = _adaln(c_pad, w_ada, b_ada)[:, :batch].reshape(depth, batch, N_MOD, d)
    x2 = x.reshape(batch * seq, d)
    for l in range(depth):
        lw = _layer_weights(l, w_in, w_gate_up, b_gate, gla_out_norm, q_a_norm, w_q_up,
                            kv_a_norm, w_kv_up, q_norm_nope, k_norm_nope, q_norm_rope,
                            k_norm_rope, w_out, w_mlp_up, w_mlp_down)
        gq, gk, gv, go, la, q, k, vt = _pre_mixer(x2, mods[l], cos, sin, lw, seq)
        gla = _gla(gq, gk, gv, go, la, lw["gla_gain"], seq)
        mla = _mla_attention(q, k, vt, batch, seq)
        x2 = _post_mixer(x2, gla, mla, mods[l], lw, seq)
    return x2.reshape(batch, seq, d)
```

```python
import functools

import numpy as np
import jax
import jax.numpy as jnp
from jax import lax
from jax.experimental import pallas as pl
from jax.experimental.pallas import tpu as pltpu

F32 = jnp.float32
BF16 = jnp.bfloat16

D_MODEL = 1024
GLA_HEADS = 4
GLA_DK = 64
GLA_DV = 128
GLA_GATE_RANK = 16
GLA_TAU = 16.0
MLA_HEADS = 4
MLA_NOPE = 128
MLA_ROPE = 64
MLA_V = 128
MLA_Q_RANK = 256
MLA_KV_RANK = 128
ROPE_THETA = 10000.0
D_GLA_QK = GLA_HEADS * GLA_DK
D_GLA = GLA_HEADS * GLA_DV
D_MLA = MLA_HEADS * MLA_V
D_FF = 4 * D_MODEL
N_MOD = 6
RMS_EPS = 1e-6
D_IN_PADDED = 2048
MLA_QK_PAD = 256
VT_ROWS = MLA_V + 16
LOG2_E = 1.4426950408889634

LANES = 128
VMEM_LIMIT_BYTES = 56 * 1024 * 1024

PRE_TM = 1024
PRE_SLABS = 4
KV_TILE = 512
ATTN_TQ = 2048
POST_TM = 512
FF_CHUNK = 1024
GLA_CHUNK = 128
GLA_BLOCK = 512
ADA_TN = 1536
ROPE_ROWS = 2048
NEG_BIG = -1e30


def _dot(a, b):
    return jnp.dot(a, b, preferred_element_type=F32)


def _dot_nt(a, b):
    return lax.dot_general(a, b, (((1,), (1,)), ((), ())), preferred_element_type=F32)


def _dot_tn(a, b):
    return lax.dot_general(a, b, (((0,), (0,)), ((), ())), preferred_element_type=F32)


def _split2(a):
    hi = a.astype(BF16)
    lo = (a - hi.astype(F32)).astype(BF16)
    return hi, lo


def _dot_f32(a, b):
    ah, al = _split2(a)
    bh, bl = _split2(b)
    return _dot(ah, bh) + _dot(ah, bl) + _dot(al, bh)


def _rms(x):
    return x * lax.rsqrt(jnp.mean(x * x, axis=-1, keepdims=True) + RMS_EPS)


def _const_spec(shape):
    zeros = (0,) * len(shape)
    return pl.BlockSpec(shape, lambda *_: zeros, pipeline_mode=pl.Buffered(1))


def _params(*semantics):
    return pltpu.CompilerParams(dimension_semantics=semantics,
                                vmem_limit_bytes=VMEM_LIMIT_BYTES)


def _adaln_kernel(c_ref, w_ref, b_ref, o_ref):
    c = c_ref[...]
    cond = c * jax.nn.sigmoid(c)
    o_ref[0] = _dot_f32(cond, w_ref[0]) + b_ref[0]


def _adaln(c_pad, w_ada, b_ada):
    depth, d, n = w_ada.shape
    rows = c_pad.shape[0]
    return pl.pallas_call(
        _adaln_kernel,
        grid=(depth, n // ADA_TN),
        in_specs=[
            pl.BlockSpec((rows, d), lambda l, j: (0, 0)),
            pl.BlockSpec((1, d, ADA_TN), lambda l, j: (l, 0, j)),
            pl.BlockSpec((1, 1, ADA_TN), lambda l, j: (l, 0, j)),
        ],
        out_specs=pl.BlockSpec((1, rows, ADA_TN), lambda l, j: (l, 0, j)),
        out_shape=jax.ShapeDtypeStruct((depth, rows, n), F32),
        compiler_params=_params("arbitrary", "arbitrary"),
        name="adaln",
    )(c_pad, w_ada, b_ada.reshape(depth, 1, n))


def _rope_kernel(pos_ref, inv_ref, cos_ref, sin_ref):
    half = MLA_ROPE // 2
    ang = pos_ref[...].astype(F32) * inv_ref[...]
    groups = LANES // half
    block = lax.broadcasted_iota(jnp.int32, (1, LANES), 1) // half
    for table, out_ref in ((jnp.cos(ang), cos_ref), (jnp.sin(ang), sin_ref)):
        rolled = [table] + [pltpu.roll(table, half * k, 1) for k in range(1, groups)]
        for g in range(groups):
            x = rolled[(0 - g) % groups]
            for b in range(1, groups):
                x = jnp.where(block == b, rolled[(b - g) % groups], x)
            out_ref[g] = x


def _rope_tables(positions):
    t = positions.size
    half = MLA_ROPE // 2
    groups = LANES // half
    rows = t // groups
    inv_freq = ROPE_THETA ** (-jnp.arange(0, MLA_ROPE, 2, dtype=F32) / MLA_ROPE)
    pos_rep = jnp.repeat(positions.reshape(groups, rows).T, half, axis=1)
    inv_rep = jnp.tile(inv_freq, groups).reshape(1, LANES)
    tr = min(ROPE_ROWS, rows)
    cos, sin = pl.pallas_call(
        _rope_kernel,
        grid=(rows // tr,),
        in_specs=[pl.BlockSpec((tr, LANES), lambda i: (i, 0)),
                  pl.BlockSpec((1, LANES), lambda i: (0, 0))],
        out_specs=[pl.BlockSpec((groups, tr, LANES), lambda i: (0, i, 0))] * 2,
        out_shape=[jax.ShapeDtypeStruct((groups, rows, LANES), F32)] * 2,
        compiler_params=_params("arbitrary"),
        name="rope_tables",
    )(pos_rep, inv_rep)
    return cos.reshape(t, LANES), sin.reshape(t, LANES)


def _norm_matrices():
    i = np.arange(256)[:, None]
    j = np.arange(256)[None, :]
    full = np.full((256, 256), 1.0 / 256)
    seg128 = ((i // 128) == (j // 128)) / 128.0
    seg64 = ((i // 64) == (j // 64)) / 64.0
    kvt = np.where((i < 128) & (j < 128), 1.0 / 128,
                   np.where((i >= 128) & (i < 192) & (j >= 128) & (j < 192), 1.0 / 64, 0.0))
    return jnp.asarray(np.stack([full, seg128, seg64, kvt]), BF16)


G_FULL, G_SEG128, G_SEG64, G_KVT = range(4)


def _pre_kernel(x_ref, mod_ref, cos_ref, sin_ref, w_in_ref, wg_ref, bg_ref, gmat_ref,
                qan_ref, kvt_ref, qnn_ref, knn_ref, qnr_ref, wq_ref, wkv_ref,
                gq_ref, gk_ref, gv_ref, go_ref, la_ref, q_ref, k_ref, vt_ref):
    mod = mod_ref[0]
    lane = lax.broadcasted_iota(jnp.int32, (1, LANES), 1)
    low = lane < MLA_ROPE
    first_half = (lane % MLA_ROPE) < (MLA_ROPE // 2)
    scale = (MLA_NOPE + MLA_ROPE) ** -0.5 * LOG2_E
    slab = x_ref.shape[0] // PRE_SLABS
    for si in range(PRE_SLABS):
        _pre_slab(slab * si, slab, mod, low, first_half, scale,
                  x_ref, cos_ref, sin_ref, w_in_ref, wg_ref, bg_ref, gmat_ref, qan_ref, kvt_ref,
                  qnn_ref, knn_ref, qnr_ref, wq_ref, wkv_ref, gq_ref, gk_ref, gv_ref, go_ref,
                  la_ref, q_ref, k_ref, vt_ref)


def _pre_slab(r0, nrows, mod, low, first_half, scale, x_ref, cos_ref, sin_ref, w_in_ref, wg_ref,
              bg_ref, gmat_ref, qan_ref, kvt_ref, qnn_ref, knn_ref, qnr_ref, wq_ref, wkv_ref,
              gq_ref, gk_ref, gv_ref, go_ref, la_ref, q_ref, k_ref, vt_ref):
    rs = slice(r0, r0 + nrows)
    h = _rms(x_ref[rs, :]) * (1.0 + mod[1:2, :]) + mod[0:1, :]
    proj = _dot(h.astype(BF16), w_in_ref[...])

    def seg_rms(y, g):
        outs = []
        for c in range(y.shape[1] // 256):
            blk = y[:, 256 * c:256 * (c + 1)]
            ms = _dot((blk * blk).astype(BF16), gmat_ref[g])
            outs.append(blk * lax.rsqrt(ms + RMS_EPS))
        return outs[0] if len(outs) == 1 else jnp.concatenate(outs, axis=1)

    gq_ref[rs, :] = proj[:, 0:256] * (GLA_DK ** -0.5)
    gk_ref[rs, :] = proj[:, 256:512]
    gv_ref[rs, :] = proj[:, 512:1024].astype(BF16)
    go_ref[rs, :] = proj[:, 1024:1536].astype(BF16)
    tail = proj[:, 1920:2048]
    z = _dot_f32(tail, wg_ref[...]) + bg_ref[...]
    log_sig = jnp.minimum(z, 0.0) - jnp.log1p(jnp.exp(-jnp.abs(z)))
    la_ref[rs, :] = log_sig * (LOG2_E / GLA_TAU)

    cos = cos_ref[rs, :]
    sin = jnp.where(first_half, -sin_ref[rs, :], sin_ref[rs, :])

    def rope(y):
        rot = jnp.where(first_half, pltpu.roll(y, LANES - MLA_ROPE // 2, 1),
                        pltpu.roll(y, MLA_ROPE // 2, 1))
        return y * cos + rot * sin

    mq = seg_rms(proj[:, 1536:1792], G_FULL) * qan_ref[...]
    qh = _dot(mq.astype(BF16), wq_ref[...])
    qn = seg_rms(qh[:, 0:512], G_SEG128) * (qnn_ref[...] * scale)
    qr = seg_rms(qh[:, 512:768], G_SEG64) * qnr_ref[...]
    q_rope = [rope(qr[:, LANES * p:LANES * (p + 1)]) * scale for p in range(MLA_HEADS // 2)]
    for hd in range(MLA_HEADS):
        q_ref[rs, MLA_QK_PAD * hd:MLA_QK_PAD * hd + LANES] = qn[:, LANES * hd:LANES * (hd + 1)].astype(BF16)
        keep = low if hd % 2 == 0 else jnp.logical_not(low)
        q_ref[rs, MLA_QK_PAD * hd + LANES:MLA_QK_PAD * (hd + 1)] = jnp.where(
            keep, q_rope[hd // 2], 0.0).astype(BF16)

    kvt = seg_rms(proj[:, 1792:2048], G_KVT) * kvt_ref[...]
    kv = _dot(kvt[:, 0:MLA_KV_RANK].astype(BF16), wkv_ref[...])
    vt = kv[:, 512:1024].T
    ones_rows = jnp.where(
        lax.broadcasted_iota(jnp.int32, (VT_ROWS - MLA_V, nrows), 0) == 0, 1.0, 0.0)
    tile, c0 = r0 // KV_TILE, r0 % KV_TILE
    for hd in range(MLA_HEADS):
        vt_ref[tile, VT_ROWS * hd:VT_ROWS * hd + MLA_V, c0:c0 + nrows] = (
            vt[MLA_V * hd:MLA_V * (hd + 1)].astype(BF16))
        vt_ref[tile, VT_ROWS * hd + MLA_V:VT_ROWS * (hd + 1), c0:c0 + nrows] = ones_rows.astype(BF16)
    kn = seg_rms(kv[:, 0:512], G_SEG128) * knn_ref[...]
    kr_even = rope(kvt[:, MLA_KV_RANK:])
    kr_odd = pltpu.roll(kr_even, MLA_ROPE, 1)
    for hd in range(MLA_HEADS):
        k_ref[rs, MLA_QK_PAD * hd:MLA_QK_PAD * hd + LANES] = kn[:, LANES * hd:LANES * (hd + 1)].astype(BF16)
        kr = kr_even if hd % 2 == 0 else kr_odd
        k_ref[rs, MLA_QK_PAD * hd + LANES:MLA_QK_PAD * (hd + 1)] = kr.astype(BF16)


def _pre_mixer(x2, mod, cos, sin, lw, seq):
    t = x2.shape[0]
    tm = min(PRE_TM, seq)
    per_batch = seq // tm
    kv_tile = min(KV_TILE, seq)
    row = lambda i: (i, 0)
    out_widths = [(D_GLA_QK, F32), (D_GLA_QK, F32), (D_GLA, BF16), (D_GLA, BF16), (D_GLA_QK, F32),
                  (MLA_HEADS * MLA_QK_PAD, BF16), (MLA_HEADS * MLA_QK_PAD, BF16)]
    vt_rows = MLA_HEADS * VT_ROWS
    return pl.pallas_call(
        _pre_kernel,
        grid=(t // tm,),
        in_specs=[
            pl.BlockSpec((tm, D_MODEL), row),
            pl.BlockSpec((1, N_MOD, D_MODEL), lambda i: (i // per_batch, 0, 0)),
            pl.BlockSpec((tm, LANES), row),
            pl.BlockSpec((tm, LANES), row),
            _const_spec((D_MODEL, D_IN_PADDED)),
            _const_spec((LANES, D_GLA_QK)),
            _const_spec((1, D_GLA_QK)),
            _const_spec((4, 256, 256)),
            _const_spec((1, MLA_Q_RANK)),
            _const_spec((1, 256)),
            _const_spec((1, MLA_HEADS * MLA_NOPE)),
            _const_spec((1, MLA_HEADS * MLA_NOPE)),
            _const_spec((1, MLA_HEADS * MLA_ROPE)),
            _const_spec((MLA_Q_RANK, MLA_HEADS * (MLA_NOPE + MLA_ROPE))),
            _const_spec((MLA_KV_RANK, MLA_HEADS * (MLA_NOPE + MLA_V))),
        ],
        out_specs=[pl.BlockSpec((tm, w), row) for w, _ in out_widths]
        + [pl.BlockSpec((tm // kv_tile, vt_rows, kv_tile), lambda i: (i, 0, 0))],
        out_shape=[jax.ShapeDtypeStruct((t, w), dt) for w, dt in out_widths]
        + [jax.ShapeDtypeStruct((t // kv_tile, vt_rows, kv_tile), BF16)],
        compiler_params=_params("arbitrary"),
        name="pre_mixer",
    )(x2, mod, cos, sin, lw["w_in"], lw["wg"], lw["bg"], _norm_matrices(), lw["qan"], lw["kvt"],
      lw["qnn"], lw["knn"], lw["qnr"], lw["wq"], lw["wkv"])


def _gla_constants(c):
    t = np.arange(c)[:, None]
    u = np.arange(c)[None, :]
    sums = []
    masks = [t == u]
    s = 1
    while s < c:
        same = (t // s) == (u // s)
        odd = ((t // s) % 2) == 1
        sums.append(np.where(odd, same & (u <= t), same & (u > t)))
        masks.append(odd & ((u // s) == (t // s) - 1))
        s *= 2
    sums.append(u <= t)
    masks = np.stack(masks)
    return (jnp.asarray(np.concatenate(sums, 0), BF16),
            jnp.asarray(np.concatenate([masks, masks], axis=-1), F32))


def _gla_kernel(q_ref, k_ref, v_ref, go_ref, la_ref, msum_ref, mask_ref, gain_ref,
                o_ref, state_ref, *, chunk, n_chunks, blocks_per_seq):
    n_levels = mask_ref.shape[0]
    pairs = GLA_HEADS // 2

    @pl.when(pl.program_id(0) % blocks_per_seq == 0)
    def _():
        state_ref[...] = jnp.zeros_like(state_ref)

    low = lax.broadcasted_iota(jnp.int32, (1, LANES), 1) < GLA_DK
    first_v = lax.broadcasted_iota(jnp.int32, (1, 2 * GLA_DV), 1) < GLA_DV
    srow = lax.broadcasted_iota(jnp.int32, (2 * GLA_DV, LANES), 0) // GLA_DV
    scol = lax.broadcasted_iota(jnp.int32, (2 * GLA_DV, LANES), 1) // GLA_DK
    diag_blocks = srow == scol

    def one_chunk(ci):
        rows = slice(ci * chunk, (ci + 1) * chunk)
        dsum = _dot(msum_ref[...], la_ref[rows, :].astype(BF16))
        b = dsum[(n_levels - 1) * chunk:n_levels * chunk]
        b_last = b[chunk - 1:chunk, :]
        q = q_ref[rows, :]
        k = k_ref[rows, :]
        q_dec = (q * jnp.exp2(b)).astype(BF16)
        k_dec = (k * jnp.exp2(b_last - b)).astype(BF16)
        decay = jnp.exp2(b_last)

        for p in range(pairs):
            cols = slice(LANES * p, LANES * (p + 1))
            q_b = q[:, cols].astype(BF16)
            k_b = k[:, cols].astype(BF16)
            zero = jnp.zeros_like(k_b)
            k_lo = jnp.where(low, k_b, zero)
            k_hi = jnp.where(low, zero, k_b)
            att = jnp.zeros((chunk, 2 * chunk), F32)
            for lv in range(n_levels):
                if lv == 0:
                    qs, ks = q_b, jnp.concatenate([k_lo, k_hi], axis=0)
                else:
                    e = jnp.exp2(dsum[(lv - 1) * chunk:lv * chunk, cols]).astype(BF16)
                    qs, ks = q_b * e, jnp.concatenate([k_lo * e, k_hi * e], axis=0)
                att = att + _dot_nt(qs, ks) * mask_ref[lv]

            st = state_ref[p]
            v_p = v_ref[rows, 2 * GLA_DV * p:2 * GLA_DV * (p + 1)]
            vz = jnp.zeros_like(v_p)
            v_bd = jnp.concatenate([jnp.where(first_v, v_p, vz), jnp.where(first_v, vz, v_p)], axis=0)
            o_pair = _dot_nt(q_dec[:, cols], st.astype(BF16)) + _dot(att.astype(BF16), v_bd)
            kv = _dot_tn(v_p, k_dec[:, cols])
            state_ref[p] = jnp.where(diag_blocks, decay[:, cols] * st + kv, 0.0)

            for hh in range(2):
                hc = slice(GLA_DV * (2 * p + hh), GLA_DV * (2 * p + hh + 1))
                g = go_ref[rows, hc].astype(F32)
                o = _rms(o_pair[:, GLA_DV * hh:GLA_DV * (hh + 1)]) * gain_ref[...]
                o_ref[rows, hc] = (o * (g * jax.nn.sigmoid(g))).astype(BF16)

    for ci in range(n_chunks):
        one_chunk(ci)


def _gla(gq, gk, gv, go, la, gain, seq):
    t = gq.shape[0]
    chunk = min(GLA_CHUNK, seq)
    tb = min(GLA_BLOCK, seq)
    msum, masks = _gla_constants(chunk)
    row = lambda i: (i, 0)
    kern = functools.partial(_gla_kernel, chunk=chunk, n_chunks=tb // chunk,
                             blocks_per_seq=seq // tb)
    return pl.pallas_call(
        kern,
        grid=(t // tb,),
        in_specs=[
            pl.BlockSpec((tb, D_GLA_QK), row),
            pl.BlockSpec((tb, D_GLA_QK), row),
            pl.BlockSpec((tb, D_GLA), row),
            pl.BlockSpec((tb, D_GLA), row),
            pl.BlockSpec((tb, D_GLA_QK), row),
            _const_spec(msum.shape),
            _const_spec(masks.shape),
            _const_spec((1, GLA_DV)),
        ],
        out_specs=pl.BlockSpec((tb, D_GLA), row),
        out_shape=jax.ShapeDtypeStruct((t, D_GLA), BF16),
        scratch_shapes=[pltpu.VMEM((GLA_HEADS // 2, 2 * GLA_DV, LANES), F32)],
        compiler_params=_params("arbitrary"),
        name="gla_scan",
    )(gq, gk, gv, go, la, msum, masks, gain)


def _attn_kernel(q_ref, k_ref, vt_ref, o_ref, s_ref, s2_ref, smax_ref, smax2_ref, m_ref, acc_ref,
                 *, tq, tk):
    qi = pl.program_id(2)
    ratio = tq // tk
    q_t = q_ref[...].astype(F32).T.astype(BF16)
    m_ref[...] = jnp.full_like(m_ref, NEG_BIG)
    acc_ref[...] = jnp.zeros_like(acc_ref)

    def scores(j, c0):
        rows = pl.ds(pl.multiple_of(j * tk, tk), tk)
        return _dot(k_ref[rows, :], q_t[:, c0:])

    def softmax_update(s, s_max, c0):
        m_prev = m_ref[:, c0:]
        m_new = jnp.maximum(m_prev, s_max)
        alpha = jnp.exp2(m_prev - m_new)
        p = jnp.exp2((s - m_new).astype(BF16))
        m_ref[:, c0:] = m_new
        return alpha, p

    def accumulate(j, alpha, p, c0):
        acc_ref[:, c0:] = alpha * acc_ref[:, c0:] + _dot(vt_ref[j], p)

    def produce(j, dst_ref, dst_max_ref):
        s = scores(j, 0)
        dst_ref[...] = s
        dst_max_ref[...] = jnp.max(s, axis=0, keepdims=True)

    def advance(j, cur_ref, cur_max_ref, nxt_ref, nxt_max_ref):
        produce(j + 1, nxt_ref, nxt_max_ref)
        alpha, p = softmax_update(cur_ref[...], cur_max_ref[...], 0)
        accumulate(j, alpha, p, 0)

    produce(0, s_ref, smax_ref)

    def body(jj, carry):
        advance(2 * jj, s_ref, smax_ref, s2_ref, smax2_ref)
        advance(2 * jj + 1, s2_ref, smax2_ref, s_ref, smax_ref)
        return carry

    lax.fori_loop(0, (qi * ratio) // 2, body, 0)

    bufs = (s_ref, s2_ref)
    for d in range(ratio):
        c0 = d * tk
        width = tq - c0
        cur = bufs[d % 2]
        if d + 1 < ratio:
            bufs[(d + 1) % 2][:, 0:width - tk] = scores(qi * ratio + d + 1, c0 + tk)
        kr = lax.broadcasted_iota(jnp.int32, (tk, width), 0)
        qc = lax.broadcasted_iota(jnp.int32, (tk, width), 1)
        s = jnp.where(kr <= qc, cur[:, 0:width], NEG_BIG)
        alpha, p = softmax_update(s, jnp.max(s, axis=0, keepdims=True), c0)
        accumulate(qi * ratio + d, alpha, p, c0)

    out_t = acc_ref[0:MLA_V, :] * (1.0 / acc_ref[MLA_V:MLA_V + 1, :])
    o_ref[...] = out_t.T.astype(BF16)


def _mla_attention(q, k, vt, batch, seq):
    t = q.shape[0]
    tk = vt.shape[2]
    tq = min(ATTN_TQ, seq)
    assert tq % tk == 0 and (tq // tk) % 2 == 0 or tq == seq
    nq = seq // tq
    return pl.pallas_call(
        functools.partial(_attn_kernel, tq=tq, tk=tk),
        grid=(batch, MLA_HEADS, nq),
        in_specs=[
            pl.BlockSpec((tq, MLA_QK_PAD), lambda b, h, i: (b * nq + i, h)),
            pl.BlockSpec((seq, MLA_QK_PAD), lambda b, h, i: (b, h)),
            pl.BlockSpec((seq // tk, VT_ROWS, tk), lambda b, h, i: (b, h, 0)),
        ],
        out_specs=pl.BlockSpec((tq, MLA_V), lambda b, h, i: (b * nq + i, h)),
        out_shape=jax.ShapeDtypeStruct((t, D_MLA), BF16),
        scratch_shapes=[pltpu.VMEM((tk, tq), F32), pltpu.VMEM((tk, tq), F32),
                        pltpu.VMEM((1, tq), F32), pltpu.VMEM((1, tq), F32),
                        pltpu.VMEM((1, tq), F32), pltpu.VMEM((VT_ROWS, tq), F32)],
        compiler_params=_params("arbitrary", "arbitrary", "arbitrary"),
        name="mla_attention",
    )(q, k, vt)


def _post_kernel(x_ref, gla_ref, mla_ref, mod_ref, wo_ref, w1_ref, w2_ref, o_ref):
    mod = mod_ref[0]
    mix = _dot(gla_ref[...], wo_ref[0:D_GLA, :]) + _dot(mla_ref[...], wo_ref[D_GLA:, :])
    x1 = x_ref[...] + mod[2:3, :] * mix
    h = (_rms(x1) * (1.0 + mod[4:5, :]) + mod[3:4, :]).astype(BF16)
    acc = jnp.zeros_like(x1)
    for c in range(D_FF // FF_CHUNK):
        cols = slice(FF_CHUNK * c, FF_CHUNK * (c + 1))
        u = jnp.maximum(_dot(h, w1_ref[:, cols]), 0.0)
        acc = acc + _dot((u * u).astype(BF16), w2_ref[cols, :])
    o_ref[...] = x1 + mod[5:6, :] * acc


def _post_mixer(x2, gla, mla, mod, lw, seq):
    t = x2.shape[0]
    tm = min(POST_TM, seq)
    per_batch = seq // tm
    row = lambda i: (i, 0)
    return pl.pallas_call(
        _post_kernel,
        grid=(t // tm,),
        in_specs=[
            pl.BlockSpec((tm, D_MODEL), row),
            pl.BlockSpec((tm, D_GLA), row),
            pl.BlockSpec((tm, D_MLA), row),
            pl.BlockSpec((1, N_MOD, D_MODEL), lambda i: (i // per_batch, 0, 0)),
            _const_spec((D_GLA + D_MLA, D_MODEL)),
            _const_spec((D_MODEL, D_FF)),
            _const_spec((D_FF, D_MODEL)),
        ],
        out_specs=pl.BlockSpec((tm, D_MODEL), row),
        out_shape=jax.ShapeDtypeStruct((t, D_MODEL), F32),
        compiler_params=_params("arbitrary"),
        name="post_mixer",
    )(x2, gla, mla, mod, lw["wo"], lw["w1"], lw["w2"])


def _layer_weights(l, w_in, w_gate_up, b_gate, gla_out_norm, q_a_norm, w_q_up, kv_a_norm,
                   w_kv_up, q_norm_nope, k_norm_nope, q_norm_rope, k_norm_rope, w_out,
                   w_mlp_up, w_mlp_down):
    wi = w_in[l]
    w_in_p = jnp.concatenate(
        [wi[:, 0:1536], wi[:, 1552:1808], wi[:, 1808:1936], wi[:, 1936:2000], wi[:, 1536:1552],
         jnp.zeros((D_MODEL, D_IN_PADDED - 2000), F32)], axis=1).astype(BF16)
    wg = jnp.zeros((LANES, D_GLA_QK), F32).at[MLA_ROPE:MLA_ROPE + GLA_GATE_RANK].set(w_gate_up[l])
    wq = w_q_up[l].reshape(MLA_Q_RANK, MLA_HEADS, MLA_NOPE + MLA_ROPE)
    wq = jnp.concatenate([wq[:, :, :MLA_NOPE].reshape(MLA_Q_RANK, -1),
                          wq[:, :, MLA_NOPE:].reshape(MLA_Q_RANK, -1)], axis=1).astype(BF16)
    wkv = w_kv_up[l].reshape(MLA_KV_RANK, MLA_HEADS, MLA_NOPE + MLA_V)
    wkv = jnp.concatenate([wkv[:, :, :MLA_NOPE].reshape(MLA_KV_RANK, -1),
                           wkv[:, :, MLA_NOPE:].reshape(MLA_KV_RANK, -1)], axis=1).astype(BF16)
    return dict(
        w_in=w_in_p, wg=wg, bg=b_gate[l].reshape(1, -1),
        gla_gain=gla_out_norm[l].reshape(1, -1),
        qan=q_a_norm[l].reshape(1, -1),
        kvt=jnp.concatenate([kv_a_norm[l], k_norm_rope[l],
                             jnp.zeros((LANES - MLA_ROPE,), F32)]).reshape(1, -1),
        qnn=jnp.tile(q_norm_nope[l], MLA_HEADS).reshape(1, -1),
        knn=jnp.tile(k_norm_nope[l], MLA_HEADS).reshape(1, -1),
        qnr=jnp.tile(q_norm_rope[l], MLA_HEADS).reshape(1, -1),
        wq=wq, wkv=wkv, wo=w_out[l].astype(BF16),
        w1=w_mlp_up[l].astype(BF16), w2=w_mlp_down[l].astype(BF16))


def kernel(x, c, positions, w_ada, b_ada, w_in, w_gate_up, b_gate, gla_out_norm, q_a_norm, w_q_up, kv_a_norm, w_kv_up, q_norm_nope, k_norm_nope, q_norm_rope, k_norm_rope, w_out, w_mlp_up, w_mlp_down):
    batch, seq, d = x.shape
    depth = w_ada.shape[0]
    cos, sin = _rope_tables(positions)
    c_pad = jnp.concatenate([c, jnp.zeros((8 - batch % 8 if batch % 8 else 0, d), F32)], axis=0)
    mods = _adaln(c_pad, w_ada, b_ada)[:, :batch].reshape(depth, batch, N_MOD, d)
    x2 = x.reshape(batch * seq, d)
    for l in range(depth):
        lw = _layer_weights(l, w_in, w_gate_up, b_gate, gla_out_norm, q_a_norm, w_q_up,
                            kv_a_norm, w_kv_up, q_norm_nope, k_norm_nope, q_norm_rope,
                            k_norm_rope, w_out, w_mlp_up, w_mlp_down)
        gq, gk, gv, go, la, q, k, vt = _pre_mixer(x2, mods[l], cos, sin, lw, seq)
        gla = _gla(gq, gk, gv, go, la, lw["gla_gain"], seq)
        mla = _mla_attention(q, k, vt, batch, seq)
        x2 = _post_mixer(x2, gla, mla, mods[l], lw, seq)
    return x2.reshape(batch, seq, d)
```

```python
import functools

import numpy as np
import jax
import jax.numpy as jnp
from jax import lax
from jax.experimental import pallas as pl
from jax.experimental.pallas import tpu as pltpu

F32 = jnp.float32
BF16 = jnp.bfloat16

D_MODEL = 1024
GLA_HEADS = 4
GLA_DK = 64
GLA_DV = 128
GLA_GATE_RANK = 16
GLA_TAU = 16.0
MLA_HEADS = 4
MLA_NOPE = 128
MLA_ROPE = 64
MLA_V = 128
MLA_Q_RANK = 256
MLA_KV_RANK = 128
ROPE_THETA = 10000.0
D_GLA_QK = GLA_HEADS * GLA_DK
D_GLA = GLA_HEADS * GLA_DV
D_MLA = MLA_HEADS * MLA_V
D_FF = 4 * D_MODEL
N_MOD = 6
RMS_EPS = 1e-6
D_IN_PADDED = 2048
MLA_QK_PAD = 256
VT_ROWS = MLA_V + 16
LOG2_E = 1.4426950408889634

LANES = 128
VMEM_LIMIT_BYTES = 56 * 1024 * 1024

PRE_TM = 1024
PRE_SLABS = 4
KV_TILE = 512
ATTN_TQ = 2048
POST_TM = 512
FF_CHUNK = 1024
GLA_CHUNK = 128
GLA_BLOCK = 512
ADA_TN = 1536
ROPE_ROWS = 2048
NEG_BIG = -1e30


def _dot(a, b):
    return jnp.dot(a, b, preferred_element_type=F32)


def _dot_nt(a, b):
    return lax.dot_general(a, b, (((1,), (1,)), ((), ())), preferred_element_type=F32)


def _dot_tn(a, b):
    return lax.dot_general(a, b, (((0,), (0,)), ((), ())), preferred_element_type=F32)


def _split2(a):
    hi = a.astype(BF16)
    lo = (a - hi.astype(F32)).astype(BF16)
    return hi, lo


def _dot_f32(a, b):
    ah, al = _split2(a)
    bh, bl = _split2(b)
    return _dot(ah, bh) + _dot(ah, bl) + _dot(al, bh)


def _rms(x):
    return x * lax.rsqrt(jnp.mean(x * x, axis=-1, keepdims=True) + RMS_EPS)


def _const_spec(shape):
    zeros = (0,) * len(shape)
    return pl.BlockSpec(shape, lambda *_: zeros, pipeline_mode=pl.Buffered(1))


def _params(*semantics):
    return pltpu.CompilerParams(dimension_semantics=semantics,
                                vmem_limit_bytes=VMEM_LIMIT_BYTES)


def _adaln_kernel(c_ref, w_ref, b_ref, o_ref):
    c = c_ref[...]
    cond = c * jax.nn.sigmoid(c)
    o_ref[0] = _dot_f32(cond, w_ref[0]) + b_ref[0]


def _adaln(c_pad, w_ada, b_ada):
    depth, d, n = w_ada.shape
    rows = c_pad.shape[0]
    return pl.pallas_call(
        _adaln_kernel,
        grid=(depth, n // ADA_TN),
        in_specs=[
            pl.BlockSpec((rows, d), lambda l, j: (0, 0)),
            pl.BlockSpec((1, d, ADA_TN), lambda l, j: (l, 0, j)),
            pl.BlockSpec((1, 1, ADA_TN), lambda l, j: (l, 0, j)),
        ],
        out_specs=pl.BlockSpec((1, rows, ADA_TN), lambda l, j: (l, 0, j)),
        out_shape=jax.ShapeDtypeStruct((depth, rows, n), F32),
        compiler_params=_params("arbitrary", "arbitrary"),
        name="adaln",
    )(c_pad, w_ada, b_ada.reshape(depth, 1, n))


def _rope_kernel(pos_ref, inv_ref, cos_ref, sin_ref):
    half = MLA_ROPE // 2
    ang = pos_ref[...].astype(F32) * inv_ref[...]
    groups = LANES // half
    block = lax.broadcasted_iota(jnp.int32, (1, LANES), 1) // half
    for table, out_ref in ((jnp.cos(ang), cos_ref), (jnp.sin(ang), sin_ref)):
        rolled = [table] + [pltpu.roll(table, half * k, 1) for k in range(1, groups)]
        for g in range(groups):
            x = rolled[(0 - g) % groups]
            for b in range(1, groups):
                x = jnp.where(block == b, rolled[(b - g) % groups], x)
            out_ref[g] = x


def _rope_tables(positions):
    t = positions.size
    half = MLA_ROPE // 2
    groups = LANES // half
    rows = t // groups
    inv_freq = ROPE_THETA ** (-jnp.arange(0, MLA_ROPE, 2, dtype=F32) / MLA_ROPE)
    pos_rep = jnp.repeat(positions.reshape(groups, rows).T, half, axis=1)
    inv_rep = jnp.tile(inv_freq, groups).reshape(1, LANES)
    tr = min(ROPE_ROWS, rows)
    cos, sin = pl.pallas_call(
        _rope_kernel,
        grid=(rows // tr,),
        in_specs=[pl.BlockSpec((tr, LANES), lambda i: (i, 0)),
                  pl.BlockSpec((1, LANES), lambda i: (0, 0))],
        out_specs=[pl.BlockSpec((groups, tr, LANES), lambda i: (0, i, 0))] * 2,
        out_shape=[jax.ShapeDtypeStruct((groups, rows, LANES), F32)] * 2,
        compiler_params=_params("arbitrary"),
        name="rope_tables",
    )(pos_rep, inv_rep)
    return cos.reshape(t, LANES), sin.reshape(t, LANES)


def _norm_matrices():
    i = np.arange(256)[:, None]
    j = np.arange(256)[None, :]
    full = np.full((256, 256), 1.0 / 256)
    seg128 = ((i // 128) == (j // 128)) / 128.0
    seg64 = ((i // 64) == (j // 64)) / 64.0
    kvt = np.where((i < 128) & (j < 128), 1.0 / 128,
                   np.where((i >= 128) & (i < 192) & (j >= 128) & (j < 192), 1.0 / 64, 0.0))
    return jnp.asarray(np.stack([full, seg128, seg64, kvt]), BF16)


G_FULL, G_SEG128, G_SEG64, G_KVT = range(4)


def _pre_kernel(x_ref, mod_ref, cos_ref, sin_ref, w_in_ref, wg_ref, bg_ref, gmat_ref,
                qan_ref, kvt_ref, qnn_ref, knn_ref, qnr_ref, wq_ref, wkv_ref,
                gq_ref, gk_ref, gv_ref, go_ref, la_ref, q_ref, k_ref, vt_ref):
    mod = mod_ref[0]
    lane = lax.broadcasted_iota(jnp.int32, (1, LANES), 1)
    low = lane < MLA_ROPE
    first_half = (lane % MLA_ROPE) < (MLA_ROPE // 2)
    scale = (MLA_NOPE + MLA_ROPE) ** -0.5 * LOG2_E
    slab = x_ref.shape[0] // PRE_SLABS
    for si in range(PRE_SLABS):
        _pre_slab(slab * si, slab, mod, low, first_half, scale,
                  x_ref, cos_ref, sin_ref, w_in_ref, wg_ref, bg_ref, gmat_ref, qan_ref, kvt_ref,
                  qnn_ref, knn_ref, qnr_ref, wq_ref, wkv_ref, gq_ref, gk_ref, gv_ref, go_ref,
                  la_ref, q_ref, k_ref, vt_ref)


def _pre_slab(r0, nrows, mod, low, first_half, scale, x_ref, cos_ref, sin_ref, w_in_ref, wg_ref,
              bg_ref, gmat_ref, qan_ref, kvt_ref, qnn_ref, knn_ref, qnr_ref, wq_ref, wkv_ref,
              gq_ref, gk_ref, gv_ref, go_ref, la_ref, q_ref, k_ref, vt_ref):
    rs = slice(r0, r0 + nrows)
    h = _rms(x_ref[rs, :]) * (1.0 + mod[1:2, :]) + mod[0:1, :]
    proj = _dot(h.astype(BF16), w_in_ref[...])

    def seg_rms(y, g):
        outs = []
        for c in range(y.shape[1] // 256):
            blk = y[:, 256 * c:256 * (c + 1)]
            ms = _dot((blk * blk).astype(BF16), gmat_ref[g])
            outs.append(blk * lax.rsqrt(ms + RMS_EPS))
        return outs[0] if len(outs) == 1 else jnp.concatenate(outs, axis=1)

    gq_ref[rs, :] = proj[:, 0:256] * (GLA_DK ** -0.5)
    gk_ref[rs, :] = proj[:, 256:512]
    gv_ref[rs, :] = proj[:, 512:1024].astype(BF16)
    go_ref[rs, :] = proj[:, 1024:1536].astype(BF16)
    tail = proj[:, 1920:2048]
    z = _dot_f32(tail, wg_ref[...]) + bg_ref[...]
    log_sig = jnp.minimum(z, 0.0) - jnp.log1p(jnp.exp(-jnp.abs(z)))
    la_ref[rs, :] = log_sig * (LOG2_E / GLA_TAU)

    cos = cos_ref[rs, :]
    sin = jnp.where(first_half, -sin_ref[rs, :], sin_ref[rs, :])

    def rope(y):
        rot = jnp.where(first_half, pltpu.roll(y, LANES - MLA_ROPE // 2, 1),
                        pltpu.roll(y, MLA_ROPE // 2, 1))
        return y * cos + rot * sin

    mq = seg_rms(proj[:, 1536:1792], G_FULL) * qan_ref[...]
    qh = _dot(mq.astype(BF16), wq_ref[...])
    qn = seg_rms(qh[:, 0:512], G_SEG128) * (qnn_ref[...] * scale)
    qr = seg_rms(qh[:, 512:768], G_SEG64) * qnr_ref[...]
    q_rope = [rope(qr[:, LANES * p:LANES * (p + 1)]) * scale for p in range(MLA_HEADS // 2)]
    for hd in range(MLA_HEADS):
        q_ref[rs, MLA_QK_PAD * hd:MLA_QK_PAD * hd + LANES] = qn[:, LANES * hd:LANES * (hd + 1)].astype(BF16)
        keep = low if hd % 2 == 0 else jnp.logical_not(low)
        q_ref[rs, MLA_QK_PAD * hd + LANES:MLA_QK_PAD * (hd + 1)] = jnp.where(
            keep, q_rope[hd // 2], 0.0).astype(BF16)

    kvt = seg_rms(proj[:, 1792:2048], G_KVT) * kvt_ref[...]
    kv = _dot(kvt[:, 0:MLA_KV_RANK].astype(BF16), wkv_ref[...])
    vt = kv[:, 512:1024].T
    ones_rows = jnp.where(
        lax.broadcasted_iota(jnp.int32, (VT_ROWS - MLA_V, nrows), 0) == 0, 1.0, 0.0)
    tile, c0 = r0 // KV_TILE, r0 % KV_TILE
    for hd in range(MLA_HEADS):
        vt_ref[tile, VT_ROWS * hd:VT_ROWS * hd + MLA_V, c0:c0 + nrows] = (
            vt[MLA_V * hd:MLA_V * (hd + 1)].astype(BF16))
        vt_ref[tile, VT_ROWS * hd + MLA_V:VT_ROWS * (hd + 1), c0:c0 + nrows] = ones_rows.astype(BF16)
    kn = seg_rms(kv[:, 0:512], G_SEG128) * knn_ref[...]
    kr_even = rope(kvt[:, MLA_KV_RANK:])
    kr_odd = pltpu.roll(kr_even, MLA_ROPE, 1)
    for hd in range(MLA_HEADS):
        k_ref[rs, MLA_QK_PAD * hd:MLA_QK_PAD * hd + LANES] = kn[:, LANES * hd:LANES * (hd + 1)].astype(BF16)
        kr = kr_even if hd % 2 == 0 else kr_odd
        k_ref[rs, MLA_QK_PAD * hd + LANES:MLA_QK_PAD * (hd + 1)] = kr.astype(BF16)


def _pre_mixer(x2, mod, cos, sin, lw, seq):
    t = x2.shape[0]
    tm = min(PRE_TM, seq)
    per_batch = seq // tm
    kv_tile = min(KV_TILE, seq)
    row = lambda i: (i, 0)
    out_widths = [(D_GLA_QK, F32), (D_GLA_QK, F32), (D_GLA, BF16), (D_GLA, BF16), (D_GLA_QK, F32),
                  (MLA_HEADS * MLA_QK_PAD, BF16), (MLA_HEADS * MLA_QK_PAD, BF16)]
    vt_rows = MLA_HEADS * VT_ROWS
    return pl.pallas_call(
        _pre_kernel,
        grid=(t // tm,),
        in_specs=[
            pl.BlockSpec((tm, D_MODEL), row),
            pl.BlockSpec((1, N_MOD, D_MODEL), lambda i: (i // per_batch, 0, 0)),
            pl.BlockSpec((tm, LANES), row),
            pl.BlockSpec((tm, LANES), row),
            _const_spec((D_MODEL, D_IN_PADDED)),
            _const_spec((LANES, D_GLA_QK)),
            _const_spec((1, D_GLA_QK)),
            _const_spec((4, 256, 256)),
            _const_spec((1, MLA_Q_RANK)),
            _const_spec((1, 256)),
            _const_spec((1, MLA_HEADS * MLA_NOPE)),
            _const_spec((1, MLA_HEADS * MLA_NOPE)),
            _const_spec((1, MLA_HEADS * MLA_ROPE)),
            _const_spec((MLA_Q_RANK, MLA_HEADS * (MLA_NOPE + MLA_ROPE))),
            _const_spec((MLA_KV_RANK, MLA_HEADS * (MLA_NOPE + MLA_V))),
        ],
        out_specs=[pl.BlockSpec((tm, w), row) for w, _ in out_widths]
        + [pl.BlockSpec((tm // kv_tile, vt_rows, kv_tile), lambda i: (i, 0, 0))],
        out_shape=[jax.ShapeDtypeStruct((t, w), dt) for w, dt in out_widths]
        + [jax.ShapeDtypeStruct((t // kv_tile, vt_rows, kv_tile), BF16)],
        compiler_params=_params("arbitrary"),
        name="pre_mixer",
    )(x2, mod, cos, sin, lw["w_in"], lw["wg"], lw["bg"], _norm_matrices(), lw["qan"], lw["kvt"],
      lw["qnn"], lw["knn"], lw["qnr"], lw["wq"], lw["wkv"])


def _gla_constants(c):
    t = np.arange(c)[:, None]
    u = np.arange(c)[None, :]
    sums = []
    masks = [t == u]
    s = 1
    while s < c:
        same = (t // s) == (u // s)
        odd = ((t // s) % 2) == 1
        sums.append(np.where(odd, same & (u <= t), same & (u > t)))
        masks.append(odd & ((u // s) == (t // s) - 1))
        s *= 2
    sums.append(u <= t)
    masks = np.stack(masks)
    return (jnp.asarray(np.concatenate(sums, 0), BF16),
            jnp.asarray(np.concatenate([masks, masks], axis=-1), F32))


def _gla_kernel(q_ref, k_ref, v_ref, go_ref, la_ref, msum_ref, mask_ref, gain_ref,
                o_ref, state_ref, *, chunk, n_chunks, blocks_per_seq):
    n_levels = mask_ref.shape[0]
    pairs = GLA_HEADS // 2

    @pl.when(pl.program_id(0) % blocks_per_seq == 0)
    def _():
        state_ref[...] = jnp.zeros_like(state_ref)

    low = lax.broadcasted_iota(jnp.int32, (1, LANES), 1) < GLA_DK
    first_v = lax.broadcasted_iota(jnp.int32, (1, 2 * GLA_DV), 1) < GLA_DV
    srow = lax.broadcasted_iota(jnp.int32, (2 * GLA_DV, LANES), 0) // GLA_DV
    scol = lax.broadcasted_iota(jnp.int32, (2 * GLA_DV, LANES), 1) // GLA_DK
    diag_blocks = srow == scol

    def one_chunk(ci):
        rows = slice(ci * chunk, (ci + 1) * chunk)
        dsum = _dot(msum_ref[...], la_ref[rows, :].astype(BF16))
        b = dsum[(n_levels - 1) * chunk:n_levels * chunk]
        b_last = b[chunk - 1:chunk, :]
        q = q_ref[rows, :]
        k = k_ref[rows, :]
        q_dec = (q * jnp.exp2(b)).astype(BF16)
        k_dec = (k * jnp.exp2(b_last - b)).astype(BF16)
        decay = jnp.exp2(b_last)

        for p in range(pairs):
            cols = slice(LANES * p, LANES * (p + 1))
            q_b = q[:, cols].astype(BF16)
            k_b = k[:, cols].astype(BF16)
            zero = jnp.zeros_like(k_b)
            k_lo = jnp.where(low, k_b, zero)
            k_hi = jnp.where(low, zero, k_b)
            att = jnp.zeros((chunk, 2 * chunk), F32)
            for lv in range(n_levels):
                if lv == 0:
                    qs, ks = q_b, jnp.concatenate([k_lo, k_hi], axis=0)
                else:
                    e = jnp.exp2(dsum[(lv - 1) * chunk:lv * chunk, cols]).astype(BF16)
                    qs, ks = q_b * e, jnp.concatenate([k_lo * e, k_hi * e], axis=0)
                att = att + _dot_nt(qs, ks) * mask_ref[lv]

            st = state_ref[p]
            v_p = v_ref[rows, 2 * GLA_DV * p:2 * GLA_DV * (p + 1)]
            vz = jnp.zeros_like(v_p)
            v_bd = jnp.concatenate([jnp.where(first_v, v_p, vz), jnp.where(first_v, vz, v_p)], axis=0)
            o_pair = _dot_nt(q_dec[:, cols], st.astype(BF16)) + _dot(att.astype(BF16), v_bd)
            kv = _dot_tn(v_p, k_dec[:, cols])
            state_ref[p] = jnp.where(diag_blocks, decay[:, cols] * st + kv, 0.0)

            for hh in range(2):
                hc = slice(GLA_DV * (2 * p + hh), GLA_DV * (2 * p + hh + 1))
                g = go_ref[rows, hc].astype(F32)
                o = _rms(o_pair[:, GLA_DV * hh:GLA_DV * (hh + 1)]) * gain_ref[...]
                o_ref[rows, hc] = (o * (g * jax.nn.sigmoid(g))).astype(BF16)

    for ci in range(n_chunks):
        one_chunk(ci)


def _gla(gq, gk, gv, go, la, gain, seq):
    t = gq.shape[0]
    chunk = min(GLA_CHUNK, seq)
    tb = min(GLA_BLOCK, seq)
    msum, masks = _gla_constants(chunk)
    row = lambda i: (i, 0)
    kern = functools.partial(_gla_kernel, chunk=chunk, n_chunks=tb // chunk,
                             blocks_per_seq=seq // tb)
    return pl.pallas_call(
        kern,
        grid=(t // tb,),
        in_specs=[
            pl.BlockSpec((tb, D_GLA_QK), row),
            pl.BlockSpec((tb, D_GLA_QK), row),
            pl.BlockSpec((tb, D_GLA), row),
            pl.BlockSpec((tb, D_GLA), row),
            pl.BlockSpec((tb, D_GLA_QK), row),
            _const_spec(msum.shape),
            _const_spec(masks.shape),
            _const_spec((1, GLA_DV)),
        ],
        out_specs=pl.BlockSpec((tb, D_GLA), row),
        out_shape=jax.ShapeDtypeStruct((t, D_GLA), BF16),
        scratch_shapes=[pltpu.VMEM((GLA_HEADS // 2, 2 * GLA_DV, LANES), F32)],
        compiler_params=_params("arbitrary"),
        name="gla_scan",
    )(gq, gk, gv, go, la, msum, masks, gain)


def _attn_kernel(q_ref, k_ref, vt_ref, o_ref, s_a, s_b, smax_a, smax_b, p_a, p_b, alpha_a, alpha_b,
                 m_ref, acc_ref, *, tq, tk):
    qi = pl.program_id(2)
    ratio = tq // tk
    n_full = qi * ratio
    q_t = q_ref[...].astype(F32).T.astype(BF16)
    m_ref[...] = jnp.full_like(m_ref, NEG_BIG)
    acc_ref[...] = jnp.zeros_like(acc_ref)
    s_bufs, smax_bufs = (s_a, s_b), (smax_a, smax_b)
    p_bufs, alpha_bufs = (p_a, p_b), (alpha_a, alpha_b)

    def scores(tile, slot):
        j, c0, diag = tile
        rows = pl.ds(pl.multiple_of(j * tk, tk), tk)
        s = _dot(k_ref[rows, :], q_t[:, c0:])
        s_bufs[slot][:, 0:tq - c0] = s
        if not diag:
            smax_bufs[slot][...] = jnp.max(s, axis=0, keepdims=True)

    def softmax(tile, slot):
        _, c0, diag = tile
        width = tq - c0
        s = s_bufs[slot][:, 0:width]
        if diag:
            kr = lax.broadcasted_iota(jnp.int32, (tk, width), 0)
            qc = lax.broadcasted_iota(jnp.int32, (tk, width), 1)
            s = jnp.where(kr <= qc, s, NEG_BIG)
            s_max = jnp.max(s, axis=0, keepdims=True)
        else:
            s_max = smax_bufs[slot][...]
        m_prev = m_ref[:, c0:]
        m_new = jnp.maximum(m_prev, s_max)
        alpha_bufs[slot][:, 0:width] = jnp.exp2(m_prev - m_new)
        p_bufs[slot][:, 0:width] = jnp.exp2((s - m_new).astype(BF16))
        m_ref[:, c0:] = m_new

    def accumulate(tile, slot):
        j, c0, _ = tile
        width = tq - c0
        pv = _dot(vt_ref[j], p_bufs[slot][:, 0:width])
        acc_ref[:, c0:] = alpha_bufs[slot][:, 0:width] * acc_ref[:, c0:] + pv

    def run(tiles, produced, softmaxed):
        n = len(tiles)
        for step in range(produced, n + 2):
            if step < n:
                scores(tiles[step], step % 2)
            if softmaxed <= step - 1 < n:
                softmax(tiles[step - 1], (step - 1) % 2)
            if 0 <= step - 2 < n:
                accumulate(tiles[step - 2], step % 2)

    diag_tiles = [(n_full + d, d * tk, True) for d in range(ratio)]

    @pl.when(qi == 0)
    def _():
        run(diag_tiles, 0, 0)

    @pl.when(qi > 0)
    def _():
        full = lambda j: (j, 0, False)
        scores(full(0), 0)
        scores(full(1), 1)
        softmax(full(0), 0)

        def body(ii, carry):
            i = 2 * ii
            scores(full(i + 2), 0)
            softmax(full(i + 1), 1)
            accumulate(full(i), 0)
            scores(full(i + 3), 1)
            softmax(full(i + 2), 0)
            accumulate(full(i + 1), 1)
            return carry

        lax.fori_loop(0, n_full // 2 - 1, body, 0)
        run([full(n_full - 2), full(n_full - 1)] + diag_tiles, 2, 1)

    out_t = acc_ref[0:MLA_V, :] * (1.0 / acc_ref[MLA_V:MLA_V + 1, :])
    o_ref[...] = out_t.T.astype(BF16)


def _mla_attention(q, k, vt, batch, seq):
    t = q.shape[0]
    tk = vt.shape[2]
    tq = min(ATTN_TQ, seq)
    assert tq % tk == 0 and ((tq // tk) % 2 == 0 or tq == seq)
    nq = seq // tq
    return pl.pallas_call(
        functools.partial(_attn_kernel, tq=tq, tk=tk),
        grid=(batch, MLA_HEADS, nq),
        in_specs=[
            pl.BlockSpec((tq, MLA_QK_PAD), lambda b, h, i: (b * nq + i, h)),
            pl.BlockSpec((seq, MLA_QK_PAD), lambda b, h, i: (b, h)),
            pl.BlockSpec((seq // tk, VT_ROWS, tk), lambda b, h, i: (b, h, 0)),
        ],
        out_specs=pl.BlockSpec((tq, MLA_V), lambda b, h, i: (b * nq + i, h)),
        out_shape=jax.ShapeDtypeStruct((t, D_MLA), BF16),
        scratch_shapes=[pltpu.VMEM((tk, tq), F32), pltpu.VMEM((tk, tq), F32),
                        pltpu.VMEM((1, tq), F32), pltpu.VMEM((1, tq), F32),
                        pltpu.VMEM((tk, tq), BF16), pltpu.VMEM((tk, tq), BF16),
                        pltpu.VMEM((1, tq), F32), pltpu.VMEM((1, tq), F32),
                        pltpu.VMEM((1, tq), F32), pltpu.VMEM((VT_ROWS, tq), F32)],
        compiler_params=_params("arbitrary", "arbitrary", "arbitrary"),
        name="mla_attention",
    )(q, k, vt)


def _post_kernel(x_ref, gla_ref, mla_ref, mod_ref, wo_ref, w1_ref, w2_ref, o_ref):
    mod = mod_ref[0]
    mix = _dot(gla_ref[...], wo_ref[0:D_GLA, :]) + _dot(mla_ref[...], wo_ref[D_GLA:, :])
    x1 = x_ref[...] + mod[2:3, :] * mix
    h = (_rms(x1) * (1.0 + mod[4:5, :]) + mod[3:4, :]).astype(BF16)
    acc = jnp.zeros_like(x1)
    for c in range(D_FF // FF_CHUNK):
        cols = slice(FF_CHUNK * c, FF_CHUNK * (c + 1))
        u = jnp.maximum(_dot(h, w1_ref[:, cols]), 0.0)
        acc = acc + _dot((u * u).astype(BF16), w2_ref[cols, :])
    o_ref[...] = x1 + mod[5:6, :] * acc


def _post_mixer(x2, gla, mla, mod, lw, seq):
    t = x2.shape[0]
    tm = min(POST_TM, seq)
    per_batch = seq // tm
    row = lambda i: (i, 0)
    return pl.pallas_call(
        _post_kernel,
        grid=(t // tm,),
        in_specs=[
            pl.BlockSpec((tm, D_MODEL), row),
            pl.BlockSpec((tm, D_GLA), row),
            pl.BlockSpec((tm, D_MLA), row),
            pl.BlockSpec((1, N_MOD, D_MODEL), lambda i: (i // per_batch, 0, 0)),
            _const_spec((D_GLA + D_MLA, D_MODEL)),
            _const_spec((D_MODEL, D_FF)),
            _const_spec((D_FF, D_MODEL)),
        ],
        out_specs=pl.BlockSpec((tm, D_MODEL), row),
        out_shape=jax.ShapeDtypeStruct((t, D_MODEL), F32),
        compiler_params=_params("arbitrary"),
        name="post_mixer",
    )(x2, gla, mla, mod, lw["wo"], lw["w1"], lw["w2"])


def _layer_weights(l, w_in, w_gate_up, b_gate, gla_out_norm, q_a_norm, w_q_up, kv_a_norm,
                   w_kv_up, q_norm_nope, k_norm_nope, q_norm_rope, k_norm_rope, w_out,
                   w_mlp_up, w_mlp_down):
    wi = w_in[l]
    w_in_p = jnp.concatenate(
        [wi[:, 0:1536], wi[:, 1552:1808], wi[:, 1808:1936], wi[:, 1936:2000], wi[:, 1536:1552],
         jnp.zeros((D_MODEL, D_IN_PADDED - 2000), F32)], axis=1).astype(BF16)
    wg = jnp.zeros((LANES, D_GLA_QK), F32).at[MLA_ROPE:MLA_ROPE + GLA_GATE_RANK].set(w_gate_up[l])
    wq = w_q_up[l].reshape(MLA_Q_RANK, MLA_HEADS, MLA_NOPE + MLA_ROPE)
    wq = jnp.concatenate([wq[:, :, :MLA_NOPE].reshape(MLA_Q_RANK, -1),
                          wq[:, :, MLA_NOPE:].reshape(MLA_Q_RANK, -1)], axis=1).astype(BF16)
    wkv = w_kv_up[l].reshape(MLA_KV_RANK, MLA_HEADS, MLA_NOPE + MLA_V)
    wkv = jnp.concatenate([wkv[:, :, :MLA_NOPE].reshape(MLA_KV_RANK, -1),
                           wkv[:, :, MLA_NOPE:].reshape(MLA_KV_RANK, -1)], axis=1).astype(BF16)
    return dict(
        w_in=w_in_p, wg=wg, bg=b_gate[l].reshape(1, -1),
        gla_gain=gla_out_norm[l].reshape(1, -1),
        qan=q_a_norm[l].reshape(1, -1),
        kvt=jnp.concatenate([kv_a_norm[l], k_norm_rope[l],
                             jnp.zeros((LANES - MLA_ROPE,), F32)]).reshape(1, -1),
        qnn=jnp.tile(q_norm_nope[l], MLA_HEADS).reshape(1, -1),
        knn=jnp.tile(k_norm_nope[l], MLA_HEADS).reshape(1, -1),
        qnr=jnp.tile(q_norm_rope[l], MLA_HEADS).reshape(1, -1),
        wq=wq, wkv=wkv, wo=w_out[l].astype(BF16),
        w1=w_mlp_up[l].astype(BF16), w2=w_mlp_down[l].astype(BF16))


def kernel(x, c, positions, w_ada, b_ada, w_in, w_gate_up, b_gate, gla_out_norm, q_a_norm, w_q_up, kv_a_norm, w_kv_up, q_norm_nope, k_norm_nope, q_norm_rope, k_norm_rope, w_out, w_mlp_up, w_mlp_down):
    batch, seq, d = x.shape
    depth = w_ada.shape[0]
    cos, sin = _rope_tables(positions)
    c_pad = jnp.concatenate([c, jnp.zeros((8 - batch % 8 if batch % 8 else 0, d), F32)], axis=0)
    mods = _adaln(c_pad, w_ada, b_ada)[:, :batch].reshape(depth, batch, N_MOD, d)
    x2 = x.reshape(batch * seq, d)
    for l in range(depth):
        lw = _layer_weights(l, w_in, w_gate_up, b_gate, gla_out_norm, q_a_norm, w_q_up,
                            kv_a_norm, w_kv_up, q_norm_nope, k_norm_nope, q_norm_rope,
                            k_norm_rope, w_out, w_mlp_up, w_mlp_down)
        gq, gk, gv, go, la, q, k, vt = _pre_mixer(x2, mods[l], cos, sin, lw, seq)
        gla = _gla(gq, gk, gv, go, la, lw["gla_gain"], seq)
        mla = _mla_attention(q, k, vt, batch, seq)
        x2 = _post_mixer(x2, gla, mla, mods[l], lw, seq)
    return x2.reshape(batch, seq, d)
```

```python
import functools

import numpy as np
import jax
import jax.numpy as jnp
from jax import lax
from jax.experimental import pallas as pl
from jax.experimental.pallas import tpu as pltpu

F32 = jnp.float32
BF16 = jnp.bfloat16

D_MODEL = 1024
GLA_HEADS = 4
GLA_DK = 64
GLA_DV = 128
GLA_GATE_RANK = 16
GLA_TAU = 16.0
MLA_HEADS = 4
MLA_NOPE = 128
MLA_ROPE = 64
MLA_V = 128
MLA_Q_RANK = 256
MLA_KV_RANK = 128
ROPE_THETA = 10000.0
D_GLA_QK = GLA_HEADS * GLA_DK
D_GLA = GLA_HEADS * GLA_DV
D_MLA = MLA_HEADS * MLA_V
D_FF = 4 * D_MODEL
N_MOD = 6
RMS_EPS = 1e-6
MLA_QK_PAD = 256
P_GK = D_GLA_QK
P_GV = P_GK + D_GLA_QK
P_GO = P_GV + D_GLA
P_MQ = P_GO + D_GLA
P_KVT = P_MQ + MLA_Q_RANK
P_TAIL = P_KVT + MLA_KV_RANK
D_IN_PADDED = P_TAIL + 128
VT_ROWS = MLA_V + 16
LOG2_E = 1.4426950408889634

LANES = 128
SUBLANES = 8
VMEM_LIMIT_BYTES = 56 * 1024 * 1024

PRE_TM = 1024
PRE_SLABS = 4
KV_TILE = 512
ATTN_TQ = 2048
POST_TM = 512
FF_CHUNK = 1024
GLA_CHUNK = 128
GLA_BLOCK = 1024
ADA_TN = 1536
ROPE_ROWS = 2048
NEG_BIG = -1e30


def _dot(a, b):
    return jnp.dot(a, b, preferred_element_type=F32)


def _dot_nt(a, b):
    return lax.dot_general(a, b, (((1,), (1,)), ((), ())), preferred_element_type=F32)


def _dot_tn(a, b):
    return lax.dot_general(a, b, (((0,), (0,)), ((), ())), preferred_element_type=F32)


def _split2(a):
    hi = a.astype(BF16)
    lo = (a - hi.astype(F32)).astype(BF16)
    return hi, lo


def _dot_f32(a, b):
    ah, al = _split2(a)
    bh, bl = _split2(b)
    return _dot(ah, bh) + _dot(ah, bl) + _dot(al, bh)


def _rms(x):
    return x * lax.rsqrt(jnp.mean(x * x, axis=-1, keepdims=True) + RMS_EPS)


def _const_spec(shape):
    zeros = (0,) * len(shape)
    return pl.BlockSpec(shape, lambda *_: zeros, pipeline_mode=pl.Buffered(1))


def _params(*semantics):
    return pltpu.CompilerParams(dimension_semantics=semantics,
                                vmem_limit_bytes=VMEM_LIMIT_BYTES)


def _adaln_kernel(c_ref, w_ref, b_ref, o_ref):
    c = c_ref[...]
    cond = c * jax.nn.sigmoid(c)
    o_ref[0] = _dot_f32(cond, w_ref[0]) + b_ref[0]


def _adaln(c_pad, w_ada, b_ada):
    depth, d, n = w_ada.shape
    rows = c_pad.shape[0]
    return pl.pallas_call(
        _adaln_kernel,
        grid=(depth, n // ADA_TN),
        in_specs=[
            pl.BlockSpec((rows, d), lambda l, j: (0, 0)),
            pl.BlockSpec((1, d, ADA_TN), lambda l, j: (l, 0, j)),
            pl.BlockSpec((1, 1, ADA_TN), lambda l, j: (l, 0, j)),
        ],
        out_specs=pl.BlockSpec((1, rows, ADA_TN), lambda l, j: (l, 0, j)),
        out_shape=jax.ShapeDtypeStruct((depth, rows, n), F32),
        compiler_params=_params("arbitrary", "arbitrary"),
        name="adaln",
    )(c_pad, w_ada, b_ada.reshape(depth, 1, n))


def _rope_kernel(pos_ref, inv_ref, cos_ref, sin_ref):
    half = MLA_ROPE // 2
    ang = pos_ref[...].astype(F32) * inv_ref[...]
    groups = LANES // half
    block = lax.broadcasted_iota(jnp.int32, (1, LANES), 1) // half
    for table, out_ref in ((jnp.cos(ang), cos_ref), (jnp.sin(ang), sin_ref)):
        rolled = [table] + [pltpu.roll(table, half * k, 1) for k in range(1, groups)]
        for g in range(groups):
            x = rolled[(0 - g) % groups]
            for b in range(1, groups):
                x = jnp.where(block == b, rolled[(b - g) % groups], x)
            out_ref[g] = x


def _rope_tables(positions):
    t = positions.size
    half = MLA_ROPE // 2
    groups = LANES // half
    rows = t // groups
    inv_freq = ROPE_THETA ** (-jnp.arange(0, MLA_ROPE, 2, dtype=F32) / MLA_ROPE)
    pos_rep = jnp.repeat(positions.reshape(groups, rows).T, half, axis=1)
    inv_rep = jnp.tile(inv_freq, groups).reshape(1, LANES)
    tr = min(ROPE_ROWS, rows)
    cos, sin = pl.pallas_call(
        _rope_kernel,
        grid=(rows // tr,),
        in_specs=[pl.BlockSpec((tr, LANES), lambda i: (i, 0)),
                  pl.BlockSpec((1, LANES), lambda i: (0, 0))],
        out_specs=[pl.BlockSpec((groups, tr, LANES), lambda i: (0, i, 0))] * 2,
        out_shape=[jax.ShapeDtypeStruct((groups, rows, LANES), F32)] * 2,
        compiler_params=_params("arbitrary"),
        name="rope_tables",
    )(pos_rep, inv_rep)
    return cos.reshape(t, LANES), sin.reshape(t, LANES)


def _norm_matrices():
    i = np.arange(256)[:, None]
    j = np.arange(256)[None, :]
    full = np.full((256, 256), 1.0 / 256)
    seg128 = ((i // 128) == (j // 128)) / 128.0
    seg64 = ((i // 64) == (j // 64)) / 64.0
    kvt = np.where((i < 128) & (j < 128), 1.0 / 128,
                   np.where((i >= 128) & (i < 192) & (j >= 128) & (j < 192), 1.0 / 64, 0.0))
    return jnp.asarray(np.stack([full, seg128, seg64, kvt]), BF16)


G_FULL, G_SEG128, G_SEG64, G_KVT = range(4)


def _pre_kernel(x_ref, mod_ref, cos_ref, sin_ref, w_in_ref, wg_ref, wg_lo_ref, bg_ref, gmat_ref,
                qan_ref, kvt_ref, qnn_ref, knn_ref, qnr_ref, wq_ref, wkv_ref,
                gq_ref, gk_ref, gv_ref, go_ref, la_ref, q_ref, k_ref, vt_ref):
    mod = mod_ref[0]
    lane = lax.broadcasted_iota(jnp.int32, (1, LANES), 1)
    low = lane < MLA_ROPE
    first_half = (lane % MLA_ROPE) < (MLA_ROPE // 2)
    scale = (MLA_NOPE + MLA_ROPE) ** -0.5 * LOG2_E
    slab = x_ref.shape[0] // PRE_SLABS
    for si in range(PRE_SLABS):
        _pre_slab(slab * si, slab, mod, low, first_half, scale,
                  x_ref, cos_ref, sin_ref, w_in_ref, wg_ref, wg_lo_ref, bg_ref, gmat_ref, qan_ref,
                  kvt_ref, qnn_ref, knn_ref, qnr_ref, wq_ref, wkv_ref, gq_ref, gk_ref, gv_ref, go_ref,
                  la_ref, q_ref, k_ref, vt_ref)


def _pre_slab(r0, nrows, mod, low, first_half, scale, x_ref, cos_ref, sin_ref, w_in_ref, wg_ref,
              wg_lo_ref, bg_ref, gmat_ref, qan_ref, kvt_ref, qnn_ref, knn_ref, qnr_ref, wq_ref, wkv_ref,
              gq_ref, gk_ref, gv_ref, go_ref, la_ref, q_ref, k_ref, vt_ref):
    rs = slice(r0, r0 + nrows)
    h = _rms(x_ref[rs, :]) * (1.0 + mod[1:2, :]) + mod[0:1, :]
    proj = _dot(h.astype(BF16), w_in_ref[...])

    def seg_rms(y, g):
        outs = []
        for c in range(y.shape[1] // 256):
            blk = y[:, 256 * c:256 * (c + 1)]
            ms = _dot((blk * blk).astype(BF16), gmat_ref[g])
            outs.append(blk * lax.rsqrt(ms + RMS_EPS))
        return outs[0] if len(outs) == 1 else jnp.concatenate(outs, axis=1)

    gq_ref[rs, :] = proj[:, 0:P_GK] * (GLA_DK ** -0.5)
    gk_ref[rs, :] = proj[:, P_GK:P_GV]
    gv_ref[rs, :] = proj[:, P_GV:P_GO].astype(BF16)
    go_ref[rs, :] = proj[:, P_GO:P_MQ].astype(BF16)
    tail = proj[:, P_TAIL:]
    t_hi, t_lo = _split2(tail)
    z = (_dot(jnp.concatenate([t_hi, t_lo], axis=1), wg_ref[...]) + _dot(t_hi, wg_lo_ref[...])
         + bg_ref[...])
    log_sig = jnp.minimum(z, 0.0) - jnp.log1p(jnp.exp(-jnp.abs(z)))
    la_ref[rs, :] = log_sig * (LOG2_E / GLA_TAU)

    cos = cos_ref[rs, :]
    sin = jnp.where(first_half, -sin_ref[rs, :], sin_ref[rs, :])

    def rope(y):
        rot = jnp.where(first_half, pltpu.roll(y, LANES - MLA_ROPE // 2, 1),
                        pltpu.roll(y, MLA_ROPE // 2, 1))
        return y * cos + rot * sin

    mq = seg_rms(proj[:, P_MQ:P_KVT], G_FULL) * qan_ref[...]
    qh = _dot(mq.astype(BF16), wq_ref[...])
    qn = seg_rms(qh[:, 0:512], G_SEG128) * (qnn_ref[...] * scale)
    qr = seg_rms(qh[:, 512:768], G_SEG64) * qnr_ref[...]
    q_rope = [rope(qr[:, LANES * p:LANES * (p + 1)]) * scale for p in range(MLA_HEADS // 2)]
    for hd in range(MLA_HEADS):
        q_ref[rs, MLA_QK_PAD * hd:MLA_QK_PAD * hd + LANES] = (
            qn[:, LANES * hd:LANES * (hd + 1)].astype(BF16))
        keep = low if hd % 2 == 0 else jnp.logical_not(low)
        q_ref[rs, MLA_QK_PAD * hd + LANES:MLA_QK_PAD * (hd + 1)] = jnp.where(
            keep, q_rope[hd // 2], 0.0).astype(BF16)

    kvt = seg_rms(proj[:, P_KVT:], G_KVT) * kvt_ref[...]
    kv = _dot(kvt[:, 0:MLA_KV_RANK].astype(BF16), wkv_ref[...])
    vt = kv[:, 512:1024].T
    ones_rows = jnp.where(
        lax.broadcasted_iota(jnp.int32, (VT_ROWS - MLA_V, nrows), 0) == 0, 1.0, 0.0)
    tile, c0 = r0 // KV_TILE, r0 % KV_TILE
    for hd in range(MLA_HEADS):
        vt_ref[tile, VT_ROWS * hd:VT_ROWS * hd + MLA_V, c0:c0 + nrows] = (
            vt[MLA_V * hd:MLA_V * (hd + 1)].astype(BF16))
        vt_ref[tile, VT_ROWS * hd + MLA_V:VT_ROWS * (hd + 1), c0:c0 + nrows] = ones_rows.astype(BF16)
    kn = seg_rms(kv[:, 0:512], G_SEG128) * knn_ref[...]
    kr_even = rope(kvt[:, MLA_KV_RANK:])
    kr_odd = pltpu.roll(kr_even, MLA_ROPE, 1)
    for hd in range(MLA_HEADS):
        k_ref[rs, MLA_QK_PAD * hd:MLA_QK_PAD * hd + LANES] = (
            kn[:, LANES * hd:LANES * (hd + 1)].astype(BF16))
        kr = kr_even if hd % 2 == 0 else kr_odd
        k_ref[rs, MLA_QK_PAD * hd + LANES:MLA_QK_PAD * (hd + 1)] = kr.astype(BF16)


def _pre_mixer(x2, mod, cos, sin, lw, seq):
    t = x2.shape[0]
    tm = min(PRE_TM, seq)
    per_batch = seq // tm
    kv_tile = min(KV_TILE, seq)
    row = lambda i: (i, 0)
    out_widths = [(D_GLA_QK, F32), (D_GLA_QK, F32), (D_GLA, BF16), (D_GLA, BF16), (D_GLA_QK, F32),
                  (MLA_HEADS * MLA_QK_PAD, BF16), (MLA_HEADS * MLA_QK_PAD, BF16)]
    vt_rows = MLA_HEADS * VT_ROWS
    return pl.pallas_call(
        _pre_kernel,
        grid=(t // tm,),
        in_specs=[
            pl.BlockSpec((tm, D_MODEL), row),
            pl.BlockSpec((1, N_MOD, D_MODEL), lambda i: (i // per_batch, 0, 0)),
            pl.BlockSpec((tm, LANES), row),
            pl.BlockSpec((tm, LANES), row),
            _const_spec((D_MODEL, D_IN_PADDED)),
            _const_spec((2 * LANES, D_GLA_QK)),
            _const_spec((LANES, D_GLA_QK)),
            _const_spec((1, D_GLA_QK)),
            _const_spec((4, 256, 256)),
            _const_spec((1, MLA_Q_RANK)),
            _const_spec((1, 256)),
            _const_spec((1, MLA_HEADS * MLA_NOPE)),
            _const_spec((1, MLA_HEADS * MLA_NOPE)),
            _const_spec((1, MLA_HEADS * MLA_ROPE)),
            _const_spec((MLA_Q_RANK, MLA_HEADS * (MLA_NOPE + MLA_ROPE))),
            _const_spec((MLA_KV_RANK, MLA_HEADS * (MLA_NOPE + MLA_V))),
        ],
        out_specs=[pl.BlockSpec((tm, w), row) for w, _ in out_widths]
        + [pl.BlockSpec((tm // kv_tile, vt_rows, kv_tile), lambda i: (i, 0, 0))],
        out_shape=[jax.ShapeDtypeStruct((t, w), dt) for w, dt in out_widths]
        + [jax.ShapeDtypeStruct((t // kv_tile, vt_rows, kv_tile), BF16)],
        compiler_params=_params("arbitrary"),
        name="pre_mixer",
    )(x2, mod, cos, sin, lw["w_in"], lw["wg"], lw["wg_lo"], lw["bg"], _norm_matrices(), lw["qan"],
      lw["kvt"], lw["qnn"], lw["knn"], lw["qnr"], lw["wq"], lw["wkv"])


def _gla_constants(c):
    t = np.arange(c)[:, None]
    u = np.arange(c)[None, :]
    sums = []
    masks = [t == u]
    s = 1
    while s < c:
        same = (t // s) == (u // s)
        odd = ((t // s) % 2) == 1
        sums.append(np.where(odd, same & (u <= t), same & (u > t)))
        masks.append(odd & ((u // s) == (t // s) - 1))
        s *= 2
    sums.append(u <= t)
    masks = np.stack(masks)
    return (jnp.asarray(np.concatenate(sums, 0), BF16),
            jnp.asarray(np.concatenate([masks, masks], axis=-1), F32))


def _gla_kernel(q_ref, k_ref, v_ref, go_ref, la_ref, msum_ref, mask_ref, gain_ref,
                o_ref, state_ref, *, chunk, n_chunks, blocks_per_seq):
    n_levels = mask_ref.shape[0]
    pairs = GLA_HEADS // 2

    @pl.when(pl.program_id(0) % blocks_per_seq == 0)
    def _():
        state_ref[...] = jnp.zeros_like(state_ref)

    low = lax.broadcasted_iota(jnp.int32, (1, LANES), 1) < GLA_DK
    first_v = lax.broadcasted_iota(jnp.int32, (1, 2 * GLA_DV), 1) < GLA_DV
    srow = lax.broadcasted_iota(jnp.int32, (2 * GLA_DV, LANES), 0) // GLA_DV
    scol = lax.broadcasted_iota(jnp.int32, (2 * GLA_DV, LANES), 1) // GLA_DK
    diag_blocks = srow == scol

    def one_chunk(ci):
        rows = slice(ci * chunk, (ci + 1) * chunk)
        dsum = _dot(msum_ref[...], la_ref[rows, :].astype(BF16))
        b = dsum[(n_levels - 1) * chunk:n_levels * chunk]
        b_last = b[chunk - 1:chunk, :]
        q = q_ref[rows, :]
        k = k_ref[rows, :]
        q_dec = (q * jnp.exp2(b)).astype(BF16)
        k_dec = (k * jnp.exp2(b_last - b)).astype(BF16)
        decay = jnp.exp2(b_last)

        for p in range(pairs):
            cols = slice(LANES * p, LANES * (p + 1))
            q_b = q[:, cols].astype(BF16)
            k_b = k[:, cols].astype(BF16)
            zero = jnp.zeros_like(k_b)
            k_lo = jnp.where(low, k_b, zero)
            k_hi = jnp.where(low, zero, k_b)
            att = jnp.zeros((chunk, 2 * chunk), F32)
            for lv in range(n_levels):
                if lv == 0:
                    qs, ks = q_b, jnp.concatenate([k_lo, k_hi], axis=0)
                else:
                    e = jnp.exp2(dsum[(lv - 1) * chunk:lv * chunk, cols]).astype(BF16)
                    qs, ks = q_b * e, jnp.concatenate([k_lo * e, k_hi * e], axis=0)
                att = att + _dot_nt(qs, ks) * mask_ref[lv]

            st = state_ref[p]
            v_p = v_ref[rows, 2 * GLA_DV * p:2 * GLA_DV * (p + 1)]
            vz = jnp.zeros_like(v_p)
            v_bd = jnp.concatenate([jnp.where(first_v, v_p, vz), jnp.where(first_v, vz, v_p)], axis=0)
            o_pair = _dot_nt(q_dec[:, cols], st.astype(BF16)) + _dot(att.astype(BF16), v_bd)
            kv = _dot_tn(v_p, k_dec[:, cols])
            state_ref[p] = jnp.where(diag_blocks, decay[:, cols] * st + kv, 0.0)

            for hh in range(2):
                hc = slice(GLA_DV * (2 * p + hh), GLA_DV * (2 * p + hh + 1))
                g = go_ref[rows, hc].astype(F32)
                o = _rms(o_pair[:, GLA_DV * hh:GLA_DV * (hh + 1)]) * gain_ref[...]
                o_ref[rows, hc] = (o * (g * jax.nn.sigmoid(g))).astype(BF16)

    for ci in range(n_chunks):
        one_chunk(ci)


def _gla(gq, gk, gv, go, la, gain, seq):
    t = gq.shape[0]
    chunk = min(GLA_CHUNK, seq)
    tb = min(GLA_BLOCK, seq)
    msum, masks = _gla_constants(chunk)
    row = lambda i: (i, 0)
    kern = functools.partial(_gla_kernel, chunk=chunk, n_chunks=tb // chunk,
                             blocks_per_seq=seq // tb)
    return pl.pallas_call(
        kern,
        grid=(t // tb,),
        in_specs=[
            pl.BlockSpec((tb, D_GLA_QK), row),
            pl.BlockSpec((tb, D_GLA_QK), row),
            pl.BlockSpec((tb, D_GLA), row),
            pl.BlockSpec((tb, D_GLA), row),
            pl.BlockSpec((tb, D_GLA_QK), row),
            _const_spec(msum.shape),
            _const_spec(masks.shape),
            _const_spec((1, GLA_DV)),
        ],
        out_specs=pl.BlockSpec((tb, D_GLA), row),
        out_shape=jax.ShapeDtypeStruct((t, D_GLA), BF16),
        scratch_shapes=[pltpu.VMEM((GLA_HEADS // 2, 2 * GLA_DV, LANES), F32)],
        compiler_params=_params("arbitrary"),
        name="gla_scan",
    )(gq, gk, gv, go, la, msum, masks, gain)


def _attn_kernel(q_ref, k_ref, vt_ref, o_ref, s_a, s_b, smax_a, smax_b, p_a, p_b, alpha_a, alpha_b,
                 m_ref, acc_ref, *, tq, tk):
    qi = pl.program_id(2)
    ratio = tq // tk
    n_full = qi * ratio
    q_t = q_ref[...].astype(F32).T.astype(BF16)
    m_ref[...] = jnp.full_like(m_ref, NEG_BIG)
    acc_ref[...] = jnp.zeros_like(acc_ref)
    s_bufs, smax_bufs = (s_a, s_b), (smax_a, smax_b)
    p_bufs, alpha_bufs = (p_a, p_b), (alpha_a, alpha_b)

    def scores(tile, slot):
        j, c0, diag = tile
        rows = pl.ds(pl.multiple_of(j * tk, tk), tk)
        s = _dot(k_ref[rows, :], q_t[:, c0:])
        s_bufs[slot][:, 0:tq - c0] = s
        if not diag:
            smax_bufs[slot][...] = jnp.max(s, axis=0, keepdims=True)

    def softmax(tile, slot):
        _, c0, diag = tile
        width = tq - c0
        s = s_bufs[slot][:, 0:width]
        if diag:
            kr = lax.broadcasted_iota(jnp.int32, (tk, width), 0)
            qc = lax.broadcasted_iota(jnp.int32, (tk, width), 1)
            s = jnp.where(kr <= qc, s, NEG_BIG)
            s_max = jnp.max(s, axis=0, keepdims=True)
        else:
            s_max = smax_bufs[slot][...]
        m_prev = m_ref[:, c0:]
        m_new = jnp.maximum(m_prev, s_max)
        alpha_bufs[slot][:, 0:width] = jnp.exp2(m_prev - m_new)
        p_bufs[slot][:, 0:width] = jnp.exp2((s - m_new).astype(BF16))
        m_ref[:, c0:] = m_new

    def accumulate(tile, slot):
        j, c0, _ = tile
        width = tq - c0
        pv = _dot(vt_ref[j], p_bufs[slot][:, 0:width])
        acc_ref[:, c0:] = alpha_bufs[slot][:, 0:width] * acc_ref[:, c0:] + pv

    def run(tiles, produced, softmaxed):
        n = len(tiles)
        for step in range(produced, n + 2):
            if step < n:
                scores(tiles[step], step % 2)
            if softmaxed <= step - 1 < n:
                softmax(tiles[step - 1], (step - 1) % 2)
            if 0 <= step - 2 < n:
                accumulate(tiles[step - 2], step % 2)

    diag_tiles = [(n_full + d, d * tk, True) for d in range(ratio)]

    @pl.when(qi == 0)
    def _():
        run(diag_tiles, 0, 0)

    @pl.when(qi > 0)
    def _():
        full = lambda j: (j, 0, False)
        scores(full(0), 0)
        scores(full(1), 1)
        softmax(full(0), 0)

        def body(ii, carry):
            i = 2 * ii
            scores(full(i + 2), 0)
            softmax(full(i + 1), 1)
            accumulate(full(i), 0)
            scores(full(i + 3), 1)
            softmax(full(i + 2), 0)
            accumulate(full(i + 1), 1)
            return carry

        lax.fori_loop(0, n_full // 2 - 1, body, 0)
        run([full(n_full - 2), full(n_full - 1)] + diag_tiles, 2, 1)

    out_t = acc_ref[0:MLA_V, :] * (1.0 / acc_ref[MLA_V:MLA_V + 1, :])
    o_ref[...] = out_t.T.astype(BF16)


def _mla_attention(q, k, vt, batch, seq):
    t = q.shape[0]
    tk = vt.shape[2]
    tq = min(ATTN_TQ, seq)
    assert tq % tk == 0 and ((tq // tk) % 2 == 0 or tq == seq)
    nq = seq // tq
    return pl.pallas_call(
        functools.partial(_attn_kernel, tq=tq, tk=tk),
        grid=(batch, MLA_HEADS, nq),
        in_specs=[
            pl.BlockSpec((tq, MLA_QK_PAD), lambda b, h, i: (b * nq + i, h)),
            pl.BlockSpec((seq, MLA_QK_PAD), lambda b, h, i: (b, h)),
            pl.BlockSpec((seq // tk, VT_ROWS, tk), lambda b, h, i: (b, h, 0)),
        ],
        out_specs=pl.BlockSpec((tq, MLA_V), lambda b, h, i: (b * nq + i, h)),
        out_shape=jax.ShapeDtypeStruct((t, D_MLA), BF16),
        scratch_shapes=[pltpu.VMEM((tk, tq), F32), pltpu.VMEM((tk, tq), F32),
                        pltpu.VMEM((1, tq), F32), pltpu.VMEM((1, tq), F32),
                        pltpu.VMEM((tk, tq), BF16), pltpu.VMEM((tk, tq), BF16),
                        pltpu.VMEM((1, tq), F32), pltpu.VMEM((1, tq), F32),
                        pltpu.VMEM((1, tq), F32), pltpu.VMEM((VT_ROWS, tq), F32)],
        compiler_params=_params("arbitrary", "arbitrary", "arbitrary"),
        name="mla_attention",
    )(q, k, vt)


def _post_kernel(x_ref, gla_ref, mla_ref, mod_ref, wo_ref, w1_ref, w2_ref, o_ref):
    mod = mod_ref[0]
    mix = _dot(gla_ref[...], wo_ref[0:D_GLA, :]) + _dot(mla_ref[...], wo_ref[D_GLA:, :])
    x1 = x_ref[...] + mod[2:3, :] * mix
    h = (_rms(x1) * (1.0 + mod[4:5, :]) + mod[3:4, :]).astype(BF16)
    acc = jnp.zeros_like(x1)
    for c in range(D_FF // FF_CHUNK):
        cols = slice(FF_CHUNK * c, FF_CHUNK * (c + 1))
        u = jnp.maximum(_dot(h, w1_ref[:, cols]), 0.0)
        acc = acc + _dot((u * u).astype(BF16), w2_ref[cols, :])
    o_ref[...] = x1 + mod[5:6, :] * acc


def _post_mixer(x2, gla, mla, mod, lw, seq):
    t = x2.shape[0]
    tm = min(POST_TM, seq)
    per_batch = seq // tm
    row = lambda i: (i, 0)
    return pl.pallas_call(
        _post_kernel,
        grid=(t // tm,),
        in_specs=[
            pl.BlockSpec((tm, D_MODEL), row),
            pl.BlockSpec((tm, D_GLA), row),
            pl.BlockSpec((tm, D_MLA), row),
            pl.BlockSpec((1, N_MOD, D_MODEL), lambda i: (i // per_batch, 0, 0)),
            _const_spec((D_GLA + D_MLA, D_MODEL)),
            _const_spec((D_MODEL, D_FF)),
            _const_spec((D_FF, D_MODEL)),
        ],
        out_specs=pl.BlockSpec((tm, D_MODEL), row),
        out_shape=jax.ShapeDtypeStruct((t, D_MODEL), F32),
        compiler_params=_params("arbitrary"),
        name="post_mixer",
    )(x2, gla, mla, mod, lw["wo"], lw["w1"], lw["w2"])


def _layer_weights(l, w_in, w_gate_up, b_gate, gla_out_norm, q_a_norm, w_q_up, kv_a_norm,
                   w_kv_up, q_norm_nope, k_norm_nope, q_norm_rope, k_norm_rope, w_out,
                   w_mlp_up, w_mlp_down):
    wi = w_in[l]
    w_in_p = jnp.concatenate(
        [wi[:, 0:1536], wi[:, 1552:1808], wi[:, 1808:1936], wi[:, 1936:2000], wi[:, 1536:1552],
         jnp.zeros((D_MODEL, D_IN_PADDED - 2000), F32)], axis=1).astype(BF16)
    wg = jnp.zeros((LANES, D_GLA_QK), F32).at[MLA_ROPE:MLA_ROPE + GLA_GATE_RANK].set(w_gate_up[l])
    wg_hi, wg_lo = _split2(wg)
    wq = w_q_up[l].reshape(MLA_Q_RANK, MLA_HEADS, MLA_NOPE + MLA_ROPE)
    wq = jnp.concatenate([wq[:, :, :MLA_NOPE].reshape(MLA_Q_RANK, -1),
                          wq[:, :, MLA_NOPE:].reshape(MLA_Q_RANK, -1)], axis=1).astype(BF16)
    wkv = w_kv_up[l].reshape(MLA_KV_RANK, MLA_HEADS, MLA_NOPE + MLA_V)
    wkv = jnp.concatenate([wkv[:, :, :MLA_NOPE].reshape(MLA_KV_RANK, -1),
                           wkv[:, :, MLA_NOPE:].reshape(MLA_KV_RANK, -1)], axis=1).astype(BF16)
    return dict(
        w_in=w_in_p, wg=jnp.concatenate([wg_hi, wg_hi], axis=0), wg_lo=wg_lo,
        bg=b_gate[l].reshape(1, -1),
        gla_gain=gla_out_norm[l].reshape(1, -1),
        qan=q_a_norm[l].reshape(1, -1),
        kvt=jnp.concatenate([kv_a_norm[l], k_norm_rope[l],
                             jnp.zeros((LANES - MLA_ROPE,), F32)]).reshape(1, -1),
        qnn=jnp.tile(q_norm_nope[l], MLA_HEADS).reshape(1, -1),
        knn=jnp.tile(k_norm_nope[l], MLA_HEADS).reshape(1, -1),
        qnr=jnp.tile(q_norm_rope[l], MLA_HEADS).reshape(1, -1),
        wq=wq, wkv=wkv, wo=w_out[l].astype(BF16),
        w1=w_mlp_up[l].astype(BF16), w2=w_mlp_down[l].astype(BF16))


def kernel(x, c, positions, w_ada, b_ada, w_in, w_gate_up, b_gate, gla_out_norm, q_a_norm, w_q_up, kv_a_norm, w_kv_up, q_norm_nope, k_norm_nope, q_norm_rope, k_norm_rope, w_out, w_mlp_up, w_mlp_down):
    batch, seq, d = x.shape
    depth = w_ada.shape[0]
    cos, sin = _rope_tables(positions)
    c_pad = jnp.concatenate([c, jnp.zeros((-batch % SUBLANES, d), F32)], axis=0)
    mods = _adaln(c_pad, w_ada, b_ada)[:, :batch].reshape(depth, batch, N_MOD, d)
    x2 = x.reshape(batch * seq, d)
    for l in range(depth):
        lw = _layer_weights(l, w_in, w_gate_up, b_gate, gla_out_norm, q_a_norm, w_q_up,
                            kv_a_norm, w_kv_up, q_norm_nope, k_norm_nope, q_norm_rope,
                            k_norm_rope, w_out, w_mlp_up, w_mlp_down)
        gq, gk, gv, go, la, q, k, vt = _pre_mixer(x2, mods[l], cos, sin, lw, seq)
        gla = _gla(gq, gk, gv, go, la, lw["gla_gain"], seq)
        mla = _mla_attention(q, k, vt, batch, seq)
        x2 = _post_mixer(x2, gla, mla, mods[l], lw, seq)
    return x2.reshape(batch, seq, d)
```

```python
import functools

import numpy as np
import jax
import jax.numpy as jnp
from jax import lax
from jax.experimental import pallas as pl
from jax.experimental.pallas import tpu as pltpu

F32 = jnp.float32
BF16 = jnp.bfloat16

D_MODEL = 1024
GLA_HEADS = 4
GLA_DK = 64
GLA_DV = 128
GLA_GATE_RANK = 16
GLA_TAU = 16.0
MLA_HEADS = 4
MLA_NOPE = 128
MLA_ROPE = 64
MLA_V = 128
MLA_Q_RANK = 256
MLA_KV_RANK = 128
ROPE_THETA = 10000.0
D_GLA_QK = GLA_HEADS * GLA_DK
D_GLA = GLA_HEADS * GLA_DV
D_MLA = MLA_HEADS * MLA_V
D_FF = 4 * D_MODEL
N_MOD = 6
RMS_EPS = 1e-6
MLA_QK_PAD = 256
P_GK = D_GLA_QK
P_GV = P_GK + D_GLA_QK
P_GO = P_GV + D_GLA
P_MQ = P_GO + D_GLA
P_KVT = P_MQ + MLA_Q_RANK
P_TAIL = P_KVT + MLA_KV_RANK
D_IN_PADDED = P_TAIL + 128
VT_ROWS = MLA_V + 16
LOG2_E = 1.4426950408889634

LANES = 128
SUBLANES = 8
VMEM_LIMIT_BYTES = 56 * 1024 * 1024

PRE_TM = 1024
PRE_SLABS = 2
KV_TILE = 512
ATTN_TQ = 2048
POST_TM = 512
FF_CHUNK = 1024
GLA_CHUNK = 128
GLA_BLOCK = 2048
ADA_TN = 1536
ROPE_ROWS = 2048
NEG_BIG = -1e30


def _dot(a, b):
    return jnp.dot(a, b, preferred_element_type=F32)


def _dot_nt(a, b):
    return lax.dot_general(a, b, (((1,), (1,)), ((), ())), preferred_element_type=F32)


def _dot_tn(a, b):
    return lax.dot_general(a, b, (((0,), (0,)), ((), ())), preferred_element_type=F32)


def _split2(a):
    hi = a.astype(BF16)
    lo = (a - hi.astype(F32)).astype(BF16)
    return hi, lo


def _dot_f32(a, b):
    ah, al = _split2(a)
    bh, bl = _split2(b)
    return _dot(ah, bh) + _dot(ah, bl) + _dot(al, bh)


def _rms(x):
    return x * lax.rsqrt(jnp.mean(x * x, axis=-1, keepdims=True) + RMS_EPS)


def _const_spec(shape):
    zeros = (0,) * len(shape)
    return pl.BlockSpec(shape, lambda *_: zeros, pipeline_mode=pl.Buffered(1))


def _params(*semantics):
    return pltpu.CompilerParams(dimension_semantics=semantics,
                                vmem_limit_bytes=VMEM_LIMIT_BYTES)


def _adaln_kernel(c_ref, w_ref, b_ref, o_ref):
    c = c_ref[...]
    cond = c * jax.nn.sigmoid(c)
    o_ref[0] = _dot_f32(cond, w_ref[0]) + b_ref[0]


def _adaln(c_pad, w_ada, b_ada):
    depth, d, n = w_ada.shape
    rows = c_pad.shape[0]
    return pl.pallas_call(
        _adaln_kernel,
        grid=(depth, n // ADA_TN),
        in_specs=[
            pl.BlockSpec((rows, d), lambda l, j: (0, 0)),
            pl.BlockSpec((1, d, ADA_TN), lambda l, j: (l, 0, j)),
            pl.BlockSpec((1, 1, ADA_TN), lambda l, j: (l, 0, j)),
        ],
        out_specs=pl.BlockSpec((1, rows, ADA_TN), lambda l, j: (l, 0, j)),
        out_shape=jax.ShapeDtypeStruct((depth, rows, n), F32),
        compiler_params=_params("arbitrary", "arbitrary"),
        name="adaln",
    )(c_pad, w_ada, b_ada.reshape(depth, 1, n))


def _rope_kernel(pos_ref, inv_ref, cos_ref, sin_ref):
    half = MLA_ROPE // 2
    ang = pos_ref[...].astype(F32) * inv_ref[...]
    groups = LANES // half
    block = lax.broadcasted_iota(jnp.int32, (1, LANES), 1) // half
    for table, out_ref in ((jnp.cos(ang), cos_ref), (jnp.sin(ang), sin_ref)):
        rolled = [table] + [pltpu.roll(table, half * k, 1) for k in range(1, groups)]
        for g in range(groups):
            x = rolled[(0 - g) % groups]
            for b in range(1, groups):
                x = jnp.where(block == b, rolled[(b - g) % groups], x)
            out_ref[g] = x


def _rope_tables(positions):
    t = positions.size
    half = MLA_ROPE // 2
    groups = LANES // half
    rows = t // groups
    inv_freq = ROPE_THETA ** (-jnp.arange(0, MLA_ROPE, 2, dtype=F32) / MLA_ROPE)
    pos_rep = jnp.repeat(positions.reshape(groups, rows).T, half, axis=1)
    inv_rep = jnp.tile(inv_freq, groups).reshape(1, LANES)
    tr = min(ROPE_ROWS, rows)
    cos, sin = pl.pallas_call(
        _rope_kernel,
        grid=(rows // tr,),
        in_specs=[pl.BlockSpec((tr, LANES), lambda i: (i, 0)),
                  pl.BlockSpec((1, LANES), lambda i: (0, 0))],
        out_specs=[pl.BlockSpec((groups, tr, LANES), lambda i: (0, i, 0))] * 2,
        out_shape=[jax.ShapeDtypeStruct((groups, rows, LANES), F32)] * 2,
        compiler_params=_params("arbitrary"),
        name="rope_tables",
    )(pos_rep, inv_rep)
    return cos.reshape(t, LANES), sin.reshape(t, LANES)


def _norm_matrices():
    i = np.arange(256)[:, None]
    j = np.arange(256)[None, :]
    full = np.full((256, 256), 1.0 / 256)
    seg128 = ((i // 128) == (j // 128)) / 128.0
    seg64 = ((i // 64) == (j // 64)) / 64.0
    kvt = np.where((i < 128) & (j < 128), 1.0 / 128,
                   np.where((i >= 128) & (i < 192) & (j >= 128) & (j < 192), 1.0 / 64, 0.0))
    return jnp.asarray(np.stack([full, seg128, seg64, kvt]), BF16)


G_FULL, G_SEG128, G_SEG64, G_KVT = range(4)


def _pre_kernel(x_ref, mod_ref, cos_ref, sin_ref, w_in_ref, wg_ref, wg_lo_ref, bg_ref, gmat_ref,
                qan_ref, kvt_ref, qnn_ref, knn_ref, qnr_ref, wq_ref, wkv_ref,
                gq_ref, gk_ref, gv_ref, go_ref, la_ref, q_ref, k_ref, vt_ref):
    mod = mod_ref[0]
    lane = lax.broadcasted_iota(jnp.int32, (1, LANES), 1)
    low = lane < MLA_ROPE
    first_half = (lane % MLA_ROPE) < (MLA_ROPE // 2)
    scale = (MLA_NOPE + MLA_ROPE) ** -0.5 * LOG2_E
    slab = x_ref.shape[0] // PRE_SLABS
    for si in range(PRE_SLABS):
        _pre_slab(slab * si, slab, mod, low, first_half, scale,
                  x_ref, cos_ref, sin_ref, w_in_ref, wg_ref, wg_lo_ref, bg_ref, gmat_ref, qan_ref,
                  kvt_ref, qnn_ref, knn_ref, qnr_ref, wq_ref, wkv_ref, gq_ref, gk_ref, gv_ref, go_ref,
                  la_ref, q_ref, k_ref, vt_ref)


def _pre_slab(r0, nrows, mod, low, first_half, scale, x_ref, cos_ref, sin_ref, w_in_ref, wg_ref,
              wg_lo_ref, bg_ref, gmat_ref, qan_ref, kvt_ref, qnn_ref, knn_ref, qnr_ref, wq_ref, wkv_ref,
              gq_ref, gk_ref, gv_ref, go_ref, la_ref, q_ref, k_ref, vt_ref):
    rs = slice(r0, r0 + nrows)
    h = _rms(x_ref[rs, :]) * (1.0 + mod[1:2, :]) + mod[0:1, :]
    proj = _dot(h.astype(BF16), w_in_ref[...])

    def seg_rms(y, g):
        outs = []
        for c in range(y.shape[1] // 256):
            blk = y[:, 256 * c:256 * (c + 1)]
            ms = _dot((blk * blk).astype(BF16), gmat_ref[g])
            outs.append(blk * lax.rsqrt(ms + RMS_EPS))
        return outs[0] if len(outs) == 1 else jnp.concatenate(outs, axis=1)

    gq_ref[rs, :] = proj[:, 0:P_GK] * (GLA_DK ** -0.5)
    gk_ref[rs, :] = proj[:, P_GK:P_GV]
    gv_ref[rs, :] = proj[:, P_GV:P_GO].astype(BF16)
    go_ref[rs, :] = proj[:, P_GO:P_MQ].astype(BF16)
    tail = proj[:, P_TAIL:]
    t_hi, t_lo = _split2(tail)
    z = (_dot(jnp.concatenate([t_hi, t_lo], axis=1), wg_ref[...]) + _dot(t_hi, wg_lo_ref[...])
         + bg_ref[...])
    log_sig = jnp.minimum(z, 0.0) - jnp.log1p(jnp.exp(-jnp.abs(z)))
    la_ref[rs, :] = log_sig * (LOG2_E / GLA_TAU)

    cos = cos_ref[rs, :]
    sin = jnp.where(first_half, -sin_ref[rs, :], sin_ref[rs, :])

    def rope(y):
        rot = jnp.where(first_half, pltpu.roll(y, LANES - MLA_ROPE // 2, 1),
                        pltpu.roll(y, MLA_ROPE // 2, 1))
        return y * cos + rot * sin

    mq = seg_rms(proj[:, P_MQ:P_KVT], G_FULL) * qan_ref[...]
    qh = _dot(mq.astype(BF16), wq_ref[...])
    qn = seg_rms(qh[:, 0:512], G_SEG128) * (qnn_ref[...] * scale)
    qr = seg_rms(qh[:, 512:768], G_SEG64) * qnr_ref[...]
    q_rope = [rope(qr[:, LANES * p:LANES * (p + 1)]) * scale for p in range(MLA_HEADS // 2)]
    for hd in range(MLA_HEADS):
        q_ref[rs, MLA_QK_PAD * hd:MLA_QK_PAD * hd + LANES] = (
            qn[:, LANES * hd:LANES * (hd + 1)].astype(BF16))
        keep = low if hd % 2 == 0 else jnp.logical_not(low)
        q_ref[rs, MLA_QK_PAD * hd + LANES:MLA_QK_PAD * (hd + 1)] = jnp.where(
            keep, q_rope[hd // 2], 0.0).astype(BF16)

    kvt = seg_rms(proj[:, P_KVT:], G_KVT) * kvt_ref[...]
    kv = _dot(kvt[:, 0:MLA_KV_RANK].astype(BF16), wkv_ref[...])
    vt = kv[:, 512:1024].T
    ones_rows = jnp.where(
        lax.broadcasted_iota(jnp.int32, (VT_ROWS - MLA_V, nrows), 0) == 0, 1.0, 0.0)
    tile, c0 = r0 // KV_TILE, r0 % KV_TILE
    for hd in range(MLA_HEADS):
        vt_ref[tile, VT_ROWS * hd:VT_ROWS * hd + MLA_V, c0:c0 + nrows] = (
            vt[MLA_V * hd:MLA_V * (hd + 1)].astype(BF16))
        vt_ref[tile, VT_ROWS * hd + MLA_V:VT_ROWS * (hd + 1), c0:c0 + nrows] = ones_rows.astype(BF16)
    kn = seg_rms(kv[:, 0:512], G_SEG128) * knn_ref[...]
    kr_even = rope(kvt[:, MLA_KV_RANK:])
    kr_odd = pltpu.roll(kr_even, MLA_ROPE, 1)
    for hd in range(MLA_HEADS):
        k_ref[rs, MLA_QK_PAD * hd:MLA_QK_PAD * hd + LANES] = (
            kn[:, LANES * hd:LANES * (hd + 1)].astype(BF16))
        kr = kr_even if hd % 2 == 0 else kr_odd
        k_ref[rs, MLA_QK_PAD * hd + LANES:MLA_QK_PAD * (hd + 1)] = kr.astype(BF16)


def _pre_mixer(x2, mod, cos, sin, lw, seq):
    t = x2.shape[0]
    tm = min(PRE_TM, seq)
    per_batch = seq // tm
    kv_tile = min(KV_TILE, seq)
    row = lambda i: (i, 0)
    out_widths = [(D_GLA_QK, F32), (D_GLA_QK, F32), (D_GLA, BF16), (D_GLA, BF16), (D_GLA_QK, F32),
                  (MLA_HEADS * MLA_QK_PAD, BF16), (MLA_HEADS * MLA_QK_PAD, BF16)]
    vt_rows = MLA_HEADS * VT_ROWS
    return pl.pallas_call(
        _pre_kernel,
        grid=(t // tm,),
        in_specs=[
            pl.BlockSpec((tm, D_MODEL), row),
            pl.BlockSpec((1, N_MOD, D_MODEL), lambda i: (i // per_batch, 0, 0)),
            pl.BlockSpec((tm, LANES), row),
            pl.BlockSpec((tm, LANES), row),
            _const_spec((D_MODEL, D_IN_PADDED)),
            _const_spec((2 * LANES, D_GLA_QK)),
            _const_spec((LANES, D_GLA_QK)),
            _const_spec((1, D_GLA_QK)),
            _const_spec((4, 256, 256)),
            _const_spec((1, MLA_Q_RANK)),
            _const_spec((1, 256)),
            _const_spec((1, MLA_HEADS * MLA_NOPE)),
            _const_spec((1, MLA_HEADS * MLA_NOPE)),
            _const_spec((1, MLA_HEADS * MLA_ROPE)),
            _const_spec((MLA_Q_RANK, MLA_HEADS * (MLA_NOPE + MLA_ROPE))),
            _const_spec((MLA_KV_RANK, MLA_HEADS * (MLA_NOPE + MLA_V))),
        ],
        out_specs=[pl.BlockSpec((tm, w), row) for w, _ in out_widths]
        + [pl.BlockSpec((tm // kv_tile, vt_rows, kv_tile), lambda i: (i, 0, 0))],
        out_shape=[jax.ShapeDtypeStruct((t, w), dt) for w, dt in out_widths]
        + [jax.ShapeDtypeStruct((t // kv_tile, vt_rows, kv_tile), BF16)],
        compiler_params=_params("arbitrary"),
        name="pre_mixer",
    )(x2, mod, cos, sin, lw["w_in"], lw["wg"], lw["wg_lo"], lw["bg"], _norm_matrices(), lw["qan"],
      lw["kvt"], lw["qnn"], lw["knn"], lw["qnr"], lw["wq"], lw["wkv"])


def _gla_constants(c):
    t = np.arange(c)[:, None]
    u = np.arange(c)[None, :]
    sums = []
    masks = [t == u]
    s = 1
    while s < c:
        same = (t // s) == (u // s)
        odd = ((t // s) % 2) == 1
        sums.append(np.where(odd, same & (u <= t), same & (u > t)))
        masks.append(odd & ((u // s) == (t // s) - 1))
        s *= 2
    sums.append(u <= t)
    masks = np.stack(masks)
    return (jnp.asarray(np.concatenate(sums, 0), BF16),
            jnp.asarray(np.concatenate([masks, masks], axis=-1), F32))


def _gla_kernel(q_ref, k_ref, v_ref, go_ref, la_ref, msum_ref, mask_ref, gain_ref,
                o_ref, state_ref, *, chunk, n_chunks, blocks_per_seq):
    n_levels = mask_ref.shape[0]
    pairs = GLA_HEADS // 2

    @pl.when(pl.program_id(0) % blocks_per_seq == 0)
    def _():
        state_ref[...] = jnp.zeros_like(state_ref)

    low = lax.broadcasted_iota(jnp.int32, (1, LANES), 1) < GLA_DK
    first_v = lax.broadcasted_iota(jnp.int32, (1, 2 * GLA_DV), 1) < GLA_DV
    srow = lax.broadcasted_iota(jnp.int32, (2 * GLA_DV, LANES), 0) // GLA_DV
    scol = lax.broadcasted_iota(jnp.int32, (2 * GLA_DV, LANES), 1) // GLA_DK
    diag_blocks = srow == scol

    def one_chunk(ci):
        rows = slice(ci * chunk, (ci + 1) * chunk)
        dsum = _dot(msum_ref[...], la_ref[rows, :].astype(BF16))
        b = dsum[(n_levels - 1) * chunk:n_levels * chunk]
        b_last = b[chunk - 1:chunk, :]
        q = q_ref[rows, :]
        k = k_ref[rows, :]
        q_dec = (q * jnp.exp2(b)).astype(BF16)
        k_dec = (k * jnp.exp2(b_last - b)).astype(BF16)
        decay = jnp.exp2(b_last)

        for p in range(pairs):
            cols = slice(LANES * p, LANES * (p + 1))
            q_b = q[:, cols].astype(BF16)
            k_b = k[:, cols].astype(BF16)
            zero = jnp.zeros_like(k_b)
            k_lo = jnp.where(low, k_b, zero)
            k_hi = jnp.where(low, zero, k_b)
            att = jnp.zeros((chunk, 2 * chunk), F32)
            for lv in range(n_levels):
                if lv == 0:
                    qs, ks = q_b, jnp.concatenate([k_lo, k_hi], axis=0)
                else:
                    e = jnp.exp2(dsum[(lv - 1) * chunk:lv * chunk, cols]).astype(BF16)
                    qs, ks = q_b * e, jnp.concatenate([k_lo * e, k_hi * e], axis=0)
                att = att + _dot_nt(qs, ks) * mask_ref[lv]

            st = state_ref[p]
            v_p = v_ref[rows, 2 * GLA_DV * p:2 * GLA_DV * (p + 1)]
            vz = jnp.zeros_like(v_p)
            v_bd = jnp.concatenate([jnp.where(first_v, v_p, vz), jnp.where(first_v, vz, v_p)], axis=0)
            o_pair = _dot_nt(q_dec[:, cols], st.astype(BF16)) + _dot(att.astype(BF16), v_bd)
            kv = _dot_tn(v_p, k_dec[:, cols])
            state_ref[p] = jnp.where(diag_blocks, decay[:, cols] * st + kv, 0.0)

            for hh in range(2):
                hc = slice(GLA_DV * (2 * p + hh), GLA_DV * (2 * p + hh + 1))
                g = go_ref[rows, hc].astype(F32)
                o = _rms(o_pair[:, GLA_DV * hh:GLA_DV * (hh + 1)]) * gain_ref[...]
                o_ref[rows, hc] = (o * (g * jax.nn.sigmoid(g))).astype(BF16)

    for ci in range(n_chunks):
        one_chunk(ci)


def _gla(gq, gk, gv, go, la, gain, seq):
    t = gq.shape[0]
    chunk = min(GLA_CHUNK, seq)
    tb = min(GLA_BLOCK, seq)
    msum, masks = _gla_constants(chunk)
    row = lambda i: (i, 0)
    kern = functools.partial(_gla_kernel, chunk=chunk, n_chunks=tb // chunk,
                             blocks_per_seq=seq // tb)
    return pl.pallas_call(
        kern,
        grid=(t // tb,),
        in_specs=[
            pl.BlockSpec((tb, D_GLA_QK), row),
            pl.BlockSpec((tb, D_GLA_QK), row),
            pl.BlockSpec((tb, D_GLA), row),
            pl.BlockSpec((tb, D_GLA), row),
            pl.BlockSpec((tb, D_GLA_QK), row),
            _const_spec(msum.shape),
            _const_spec(masks.shape),
            _const_spec((1, GLA_DV)),
        ],
        out_specs=pl.BlockSpec((tb, D_GLA), row),
        out_shape=jax.ShapeDtypeStruct((t, D_GLA), BF16),
        scratch_shapes=[pltpu.VMEM((GLA_HEADS // 2, 2 * GLA_DV, LANES), F32)],
        compiler_params=_params("arbitrary"),
        name="gla_scan",
    )(gq, gk, gv, go, la, msum, masks, gain)


def _attn_kernel(q_ref, k_ref, vt_ref, o_ref, s_a, s_b, smax_a, smax_b, p_a, p_b, alpha_a, alpha_b,
                 m_ref, acc_ref, *, tq, tk):
    qi = pl.program_id(2)
    ratio = tq // tk
    n_full = qi * ratio
    q_t = q_ref[...].astype(F32).T.astype(BF16)
    m_ref[...] = jnp.full_like(m_ref, NEG_BIG)
    acc_ref[...] = jnp.zeros_like(acc_ref)
    s_bufs, smax_bufs = (s_a, s_b), (smax_a, smax_b)
    p_bufs, alpha_bufs = (p_a, p_b), (alpha_a, alpha_b)

    def scores(tile, slot):
        j, c0, diag = tile
        rows = pl.ds(pl.multiple_of(j * tk, tk), tk)
        s = _dot(k_ref[rows, :], q_t[:, c0:])
        s_bufs[slot][:, 0:tq - c0] = s
        if not diag:
            smax_bufs[slot][...] = jnp.max(s, axis=0, keepdims=True)

    def softmax(tile, slot):
        _, c0, diag = tile
        width = tq - c0
        s = s_bufs[slot][:, 0:width]
        if diag:
            kr = lax.broadcasted_iota(jnp.int32, (tk, width), 0)
            qc = lax.broadcasted_iota(jnp.int32, (tk, width), 1)
            s = jnp.where(kr <= qc, s, NEG_BIG)
            s_max = jnp.max(s, axis=0, keepdims=True)
        else:
            s_max = smax_bufs[slot][...]
        m_prev = m_ref[:, c0:]
        m_new = jnp.maximum(m_prev, s_max)
        alpha_bufs[slot][:, 0:width] = jnp.exp2(m_prev - m_new)
        p_bufs[slot][:, 0:width] = jnp.exp2((s - m_new).astype(BF16))
        m_ref[:, c0:] = m_new

    def accumulate(tile, slot):
        j, c0, _ = tile
        width = tq - c0
        pv = _dot(vt_ref[j], p_bufs[slot][:, 0:width])
        acc_ref[:, c0:] = alpha_bufs[slot][:, 0:width] * acc_ref[:, c0:] + pv

    def run(tiles, produced, softmaxed):
        n = len(tiles)
        for step in range(produced, n + 2):
            if step < n:
                scores(tiles[step], step % 2)
            if softmaxed <= step - 1 < n:
                softmax(tiles[step - 1], (step - 1) % 2)
            if 0 <= step - 2 < n:
                accumulate(tiles[step - 2], step % 2)

    diag_tiles = [(n_full + d, d * tk, True) for d in range(ratio)]

    @pl.when(qi == 0)
    def _():
        run(diag_tiles, 0, 0)

    @pl.when(qi > 0)
    def _():
        full = lambda j: (j, 0, False)
        scores(full(0), 0)
        scores(full(1), 1)
        softmax(full(0), 0)

        def body(ii, carry):
            i = 2 * ii
            scores(full(i + 2), 0)
            softmax(full(i + 1), 1)
            accumulate(full(i), 0)
            scores(full(i + 3), 1)
            softmax(full(i + 2), 0)
            accumulate(full(i + 1), 1)
            return carry

        lax.fori_loop(0, n_full // 2 - 1, body, 0)
        run([full(n_full - 2), full(n_full - 1)] + diag_tiles, 2, 1)

    out_t = acc_ref[0:MLA_V, :] * (1.0 / acc_ref[MLA_V:MLA_V + 1, :])
    o_ref[...] = out_t.T.astype(BF16)


def _mla_attention(q, k, vt, batch, seq):
    t = q.shape[0]
    tk = vt.shape[2]
    tq = min(ATTN_TQ, seq)
    assert tq % tk == 0 and ((tq // tk) % 2 == 0 or tq == seq)
    nq = seq // tq
    return pl.pallas_call(
        functools.partial(_attn_kernel, tq=tq, tk=tk),
        grid=(batch, MLA_HEADS, nq),
        in_specs=[
            pl.BlockSpec((tq, MLA_QK_PAD), lambda b, h, i: (b * nq + i, h)),
            pl.BlockSpec((seq, MLA_QK_PAD), lambda b, h, i: (b, h)),
            pl.BlockSpec((seq // tk, VT_ROWS, tk), lambda b, h, i: (b, h, 0)),
        ],
        out_specs=pl.BlockSpec((tq, MLA_V), lambda b, h, i: (b * nq + i, h)),
        out_shape=jax.ShapeDtypeStruct((t, D_MLA), BF16),
        scratch_shapes=[pltpu.VMEM((tk, tq), F32), pltpu.VMEM((tk, tq), F32),
                        pltpu.VMEM((1, tq), F32), pltpu.VMEM((1, tq), F32),
                        pltpu.VMEM((tk, tq), BF16), pltpu.VMEM((tk, tq), BF16),
                        pltpu.VMEM((1, tq), F32), pltpu.VMEM((1, tq), F32),
                        pltpu.VMEM((1, tq), F32), pltpu.VMEM((VT_ROWS, tq), F32)],
        compiler_params=_params("arbitrary", "arbitrary", "arbitrary"),
        name="mla_attention",
    )(q, k, vt)


def _post_kernel(x_ref, gla_ref, mla_ref, mod_ref, wo_ref, w1_ref, w2_ref, o_ref):
    mod = mod_ref[0]
    mix = _dot(gla_ref[...], wo_ref[0:D_GLA, :]) + _dot(mla_ref[...], wo_ref[D_GLA:, :])
    x1 = x_ref[...] + mod[2:3, :] * mix
    h = (_rms(x1) * (1.0 + mod[4:5, :]) + mod[3:4, :]).astype(BF16)
    acc = jnp.zeros_like(x1)
    for c in range(D_FF // FF_CHUNK):
        cols = slice(FF_CHUNK * c, FF_CHUNK * (c + 1))
        u = jnp.maximum(_dot(h, w1_ref[:, cols]), 0.0)
        acc = acc + _dot((u * u).astype(BF16), w2_ref[cols, :])
    o_ref[...] = x1 + mod[5:6, :] * acc


def _post_mixer(x2, gla, mla, mod, lw, seq):
    t = x2.shape[0]
    tm = min(POST_TM, seq)
    per_batch = seq // tm
    row = lambda i: (i, 0)
    return pl.pallas_call(
        _post_kernel,
        grid=(t // tm,),
        in_specs=[
            pl.BlockSpec((tm, D_MODEL), row),
            pl.BlockSpec((tm, D_GLA), row),
            pl.BlockSpec((tm, D_MLA), row),
            pl.BlockSpec((1, N_MOD, D_MODEL), lambda i: (i // per_batch, 0, 0)),
            _const_spec((D_GLA + D_MLA, D_MODEL)),
            _const_spec((D_MODEL, D_FF)),
            _const_spec((D_FF, D_MODEL)),
        ],
        out_specs=pl.BlockSpec((tm, D_MODEL), row),
        out_shape=jax.ShapeDtypeStruct((t, D_MODEL), F32),
        compiler_params=_params("arbitrary"),
        name="post_mixer",
    )(x2, gla, mla, mod, lw["wo"], lw["w1"], lw["w2"])


def _layer_weights(l, w_in, w_gate_up, b_gate, gla_out_norm, q_a_norm, w_q_up, kv_a_norm,
                   w_kv_up, q_norm_nope, k_norm_nope, q_norm_rope, k_norm_rope, w_out,
                   w_mlp_up, w_mlp_down):
    wi = w_in[l]
    w_in_p = jnp.concatenate(
        [wi[:, 0:1536], wi[:, 1552:1808], wi[:, 1808:1936], wi[:, 1936:2000], wi[:, 1536:1552],
         jnp.zeros((D_MODEL, D_IN_PADDED - 2000), F32)], axis=1).astype(BF16)
    wg = jnp.zeros((LANES, D_GLA_QK), F32).at[MLA_ROPE:MLA_ROPE + GLA_GATE_RANK].set(w_gate_up[l])
    wg_hi, wg_lo = _split2(wg)
    wq = w_q_up[l].reshape(MLA_Q_RANK, MLA_HEADS, MLA_NOPE + MLA_ROPE)
    wq = jnp.concatenate([wq[:, :, :MLA_NOPE].reshape(MLA_Q_RANK, -1),
                          wq[:, :, MLA_NOPE:].reshape(MLA_Q_RANK, -1)], axis=1).astype(BF16)
    wkv = w_kv_up[l].reshape(MLA_KV_RANK, MLA_HEADS, MLA_NOPE + MLA_V)
    wkv = jnp.concatenate([wkv[:, :, :MLA_NOPE].reshape(MLA_KV_RANK, -1),
                           wkv[:, :, MLA_NOPE:].reshape(MLA_KV_RANK, -1)], axis=1).astype(BF16)
    return dict(
        w_in=w_in_p, wg=jnp.concatenate([wg_hi, wg_hi], axis=0), wg_lo=wg_lo,
        bg=b_gate[l].reshape(1, -1),
        gla_gain=gla_out_norm[l].reshape(1, -1),
        qan=q_a_norm[l].reshape(1, -1),
        kvt=jnp.concatenate([kv_a_norm[l], k_norm_rope[l],
                             jnp.zeros((LANES - MLA_ROPE,), F32)]).reshape(1, -1),
        qnn=jnp.tile(q_norm_nope[l], MLA_HEADS).reshape(1, -1),
        knn=jnp.tile(k_norm_nope[l], MLA_HEADS).reshape(1, -1),
        qnr=jnp.tile(q_norm_rope[l], MLA_HEADS).reshape(1, -1),
        wq=wq, wkv=wkv, wo=w_out[l].astype(BF16),
        w1=w_mlp_up[l].astype(BF16), w2=w_mlp_down[l].astype(BF16))


def kernel(x, c, positions, w_ada, b_ada, w_in, w_gate_up, b_gate, gla_out_norm, q_a_norm, w_q_up, kv_a_norm, w_kv_up, q_norm_nope, k_norm_nope, q_norm_rope, k_norm_rope, w_out, w_mlp_up, w_mlp_down):
    batch, seq, d = x.shape
    depth = w_ada.shape[0]
    cos, sin = _rope_tables(positions)
    c_pad = jnp.concatenate([c, jnp.zeros((-batch % SUBLANES, d), F32)], axis=0)
    mods = _adaln(c_pad, w_ada, b_ada)[:, :batch].reshape(depth, batch, N_MOD, d)
    x2 = x.reshape(batch * seq, d)
    for l in range(depth):
        lw = _layer_weights(l, w_in, w_gate_up, b_gate, gla_out_norm, q_a_norm, w_q_up,
                            kv_a_norm, w_kv_up, q_norm_nope, k_norm_nope, q_norm_rope,
                            k_norm_rope, w_out, w_mlp_up, w_mlp_down)
        gq, gk, gv, go, la, q, k, vt = _pre_mixer(x2, mods[l], cos, sin, lw, seq)
        gla = _gla(gq, gk, gv, go, la, lw["gla_gain"], seq)
        mla = _mla_attention(q, k, vt, batch, seq)
        x2 = _post_mixer(x2, gla, mla, mods[l], lw, seq)
    return x2.reshape(batch, seq, d)
```
